```python
import math
import jax
import jax.numpy as jnp
from jax import lax
import numpy as np

D_MODEL = 1024
BATCH = 4
SEQ = 8192
DEPTH = 4

GRID_W = 64
EPS = 1e-6

A_HEADS = 8
A_KV_HEADS = 2
A_HEAD_DIM = 64
A_GROUP = A_HEADS // A_KV_HEADS
A_BLOCK = 128
ROPE_THETA = 10000.0
ROPE_FREQS = A_HEAD_DIM // 4

DN_HEADS = 8
DN_DK = 64
DN_DV = 64
DN_CONV_W = 5
DN_CHUNK = 64

D_FF = 2816
FFN_CONV_W = 3

N_BRANCH = 2
A_Q = A_HEADS * A_HEAD_DIM
A_KV = A_KV_HEADS * A_HEAD_DIM
DN_QK = DN_HEADS * DN_DK
DN_V = DN_HEADS * DN_DV
SPLIT_SIZES = (A_Q, A_KV, A_KV, DN_QK, DN_QK, DN_V, 2 * DN_HEADS, 2 * DN_HEADS, DN_V, N_BRANCH * D_MODEL)
N_IN = A_Q + 2 * A_KV + 2 * DN_QK + 2 * DN_V + 4 * DN_HEADS + N_BRANCH * D_MODEL

kernel_name = "hybrid_gridrope_gqa_gated_deltanet_convffn_encoder"


def rmsnorm(x, g):
    xf = x.astype(jnp.float32)
    y = xf * lax.rsqrt(jnp.mean(xf * xf, axis=-1, keepdims=True) + EPS)
    return (y * g.astype(jnp.float32)).astype(x.dtype)


def l2norm(x):
    return x * lax.rsqrt(jnp.sum(x * x, axis=-1, keepdims=True) + EPS)


def centred_depthwise_conv(x, w):
    k = w.shape[0]
    c = x.shape[-1]
    return lax.conv_general_dilated(
        x, w.astype(x.dtype)[:, None, :], window_strides=(1,),
        padding=[(k // 2, k // 2)], dimension_numbers=("NWC", "WIO", "NWC"),
        feature_group_count=c)


def grid_rope_tables(seq_len):
    rows_n = seq_len // GRID_W
    row = jnp.repeat(jnp.arange(rows_n), GRID_W).astype(jnp.float32)
    col = jnp.tile(jnp.arange(GRID_W), rows_n).astype(jnp.float32)
    inv_freq = ROPE_THETA ** (-jnp.arange(ROPE_FREQS, dtype=jnp.float32) / ROPE_FREQS)
    ang = jnp.stack([row[:, None] * inv_freq, col[:, None] * inv_freq], axis=1)
    return jnp.cos(ang), jnp.sin(ang)


def apply_grid_rope(x, cos, sin):
    b, s, h, d = x.shape
    xr = x.astype(jnp.float32).reshape(b, s, h, 2, 2, ROPE_FREQS)
    x1, x2 = xr[..., 0, :], xr[..., 1, :]
    c = cos[None, :, None]
    sn = sin[None, :, None]
    out = jnp.stack([x1 * c - x2 * sn, x2 * c + x1 * sn], axis=-2)
    return out.reshape(b, s, h, d).astype(x.dtype)


def grid_attention(q, k, v):
    b, s, _, dh = q.shape
    nb = s // A_BLOCK
    qb = jnp.moveaxis(q.reshape(b, nb, A_BLOCK, A_KV_HEADS, A_GROUP, dh), 1, 0)
    scale = dh ** -0.5

    def block(qi):
        sc = jnp.einsum("bqkgd,bskd->bkgqs", qi, k).astype(jnp.float32) * scale
        p = jax.nn.softmax(sc, axis=-1).astype(v.dtype)
        return jnp.einsum("bkgqs,bskd->bqkgd", p, v)

    o = lax.map(block, qb)
    return jnp.moveaxis(o, 0, 1).reshape(b, s, A_HEADS * dh)


def gated_delta_chunked(q, k, v, beta, g):
    b, s, h, dk = q.shape
    dv = v.shape[-1]
    c = DN_CHUNK
    n = s // c

    def to_chunks(t):
        t = jnp.moveaxis(t, 2, 1)
        return t.reshape((b, h, n, c) + t.shape[3:])

    q, k, v, beta, g = (to_chunks(t) for t in (q, k, v, beta, g))
    gcum = jnp.cumsum(g, axis=-1)
    idx = jnp.arange(c)
    incl = idx[:, None] >= idx[None, :]
    strict = idx[:, None] > idx[None, :]
    decay = jnp.exp(jnp.where(incl, gcum[..., :, None] - gcum[..., None, :], -jnp.inf))
    k_beta = k * beta[..., None]
    a_kk = jnp.where(strict, jnp.einsum("bhnid,bhnjd->bhnij", k_beta, k) * decay, 0.0)
    eye = jnp.eye(c, dtype=q.dtype)
    t_inv = lax.linalg.triangular_solve(eye + a_kk, jnp.broadcast_to(eye, a_kk.shape),
                                        left_side=True, lower=True, unit_diagonal=True)
    u = jnp.einsum("bhnij,bhnjd->bhnid", t_inv, v * beta[..., None])
    w = jnp.einsum("bhnij,bhnjd->bhnid", t_inv, k_beta * jnp.exp(gcum)[..., None])
    qk = jnp.einsum("bhnid,bhnjd->bhnij", q, k) * decay
    q_dec = q * jnp.exp(gcum)[..., None]
    k_dec = k * jnp.exp(gcum[..., -1:] - gcum)[..., None]
    g_tot = jnp.exp(gcum[..., -1])

    def step(state, inp):
        qd, kd, ui, wi, qki, gt = inp
        v_new = ui - jnp.einsum("bhcd,bhde->bhce", wi, state)
        o = jnp.einsum("bhcd,bhde->bhce", qd, state) + jnp.einsum("bhcj,bhje->bhce", qki, v_new)
        state = state * gt[..., None, None] + jnp.einsum("bhcd,bhce->bhde", kd, v_new)
        return state, o

    xs = tuple(jnp.moveaxis(t, 2, 0) for t in (q_dec, k_dec, u, w, qk, g_tot))
    state0 = jnp.zeros((b, h, dk, dv), q.dtype)
    _, o = lax.scan(step, state0, xs)
    o = jnp.moveaxis(o, 0, 2).reshape(b, h, s, dv)
    return jnp.moveaxis(o, 1, 2)


def bidir_gated_deltanet(q, k, v, beta, g):
    fwd = gated_delta_chunked(q, k, v, beta[:, :, 0], g[:, :, 0])
    flip = lambda t: jnp.flip(t, axis=1)
    bwd = flip(gated_delta_chunked(flip(q), flip(k), flip(v), flip(beta[:, :, 1]), flip(g[:, :, 1])))
    return fwd + bwd


def setup_inputs(seed: int = 0) -> dict:
    key = jax.random.key(seed)
    ks = jax.random.split(key, 17)
    f32 = jnp.float32

    def normal(k, shape, scale):
        return jax.random.normal(k, shape, f32) * scale

    def gain(k, shape):
        return 1.0 + 0.05 * jax.random.normal(k, shape, f32)

    dt = jnp.exp(jax.random.uniform(ks[7], (DEPTH, 2, DN_HEADS), f32, math.log(1e-3), math.log(1e-1)))
    return {
        "x": jax.random.normal(ks[0], (BATCH, SEQ, D_MODEL), f32),
        "norm_mix_g": gain(ks[1], (DEPTH, D_MODEL)),
        "w_in": normal(ks[2], (DEPTH, D_MODEL, N_IN), D_MODEL ** -0.5),
        "q_norm_g": gain(ks[3], (DEPTH, A_HEAD_DIM)),
        "k_norm_g": gain(ks[4], (DEPTH, A_HEAD_DIM)),
        "dn_conv_w": normal(ks[5], (DEPTH, DN_CONV_W, 2 * DN_QK + DN_V), DN_CONV_W ** -0.5),
        "dn_a_log": jnp.log(jax.random.uniform(ks[6], (DEPTH, 2, DN_HEADS), f32, 1.0, 16.0)),
        "dn_dt_bias": dt + jnp.log(-jnp.expm1(-dt)),
        "dn_out_norm_g": gain(ks[8], (DEPTH, DN_DV)),
        "w_o_attn": normal(ks[9], (DEPTH, A_Q, D_MODEL), A_Q ** -0.5),
        "w_o_dn": normal(ks[10], (DEPTH, DN_V, D_MODEL), DN_V ** -0.5),
        "w_out": normal(ks[11], (DEPTH, D_MODEL, D_MODEL), D_MODEL ** -0.5),
        "norm_ffn_g": gain(ks[12], (DEPTH, D_MODEL)),
        "w_up": normal(ks[13], (DEPTH, D_MODEL, 2 * D_FF), D_MODEL ** -0.5),
        "ffn_conv_w": normal(ks[14], (DEPTH, FFN_CONV_W, 2 * D_FF), FFN_CONV_W ** -0.5),
        "w_down": normal(ks[15], (DEPTH, D_FF, D_MODEL), D_FF ** -0.5),
    }


def reference(x, norm_mix_g, w_in, q_norm_g, k_norm_g, dn_conv_w, dn_a_log, dn_dt_bias,
              dn_out_norm_g, w_o_attn, w_o_dn, w_out, norm_ffn_g, w_up, ffn_conv_w, w_down):
    f32 = jnp.float32
    b, s, _ = x.shape
    cos, sin = grid_rope_tables(s)
    offsets = [int(o) for o in np.cumsum(SPLIT_SIZES)[:-1]]
    for l in range(DEPTH):
        h = rmsnorm(x, norm_mix_g[l])
        proj = h @ w_in[l]
        (qa, ka, va, qd, kd, vd, beta_logit, decay_logit, z, gate_logit) = jnp.split(proj, offsets, axis=-1)

        qa = apply_grid_rope(rmsnorm(qa.reshape(b, s, A_HEADS, A_HEAD_DIM), q_norm_g[l]), cos, sin)
        ka = apply_grid_rope(rmsnorm(ka.reshape(b, s, A_KV_HEADS, A_HEAD_DIM), k_norm_g[l]), cos, sin)
        va = va.reshape(b, s, A_KV_HEADS, A_HEAD_DIM)
        y_attn = grid_attention(qa, ka, va) @ w_o_attn[l]

        qkv = jax.nn.silu(centred_depthwise_conv(jnp.concatenate([qd, kd, vd], axis=-1), dn_conv_w[l])).astype(f32)
        qd, kd, vd = jnp.split(qkv, [DN_QK, 2 * DN_QK], axis=-1)
        qd = l2norm(qd.reshape(b, s, DN_HEADS, DN_DK)) * (DN_DK ** -0.5)
        kd = l2norm(kd.reshape(b, s, DN_HEADS, DN_DK))
        vd = vd.reshape(b, s, DN_HEADS, DN_DV)
        beta = jax.nn.sigmoid(beta_logit.astype(f32)).reshape(b, s, 2, DN_HEADS)
        g = -jnp.exp(dn_a_log[l].astype(f32)) * jax.nn.softplus(
            decay_logit.astype(f32).reshape(b, s, 2, DN_HEADS) + dn_dt_bias[l].astype(f32))
        o = bidir_gated_deltanet(qd, kd, vd, beta, g)
        o = rmsnorm(o, dn_out_norm_g[l]) * jax.nn.silu(z.astype(f32).reshape(b, s, DN_HEADS, DN_DV))
        y_dn = o.reshape(b, s, DN_V).astype(x.dtype) @ w_o_dn[l]

        gates = jax.nn.sigmoid(gate_logit).reshape(b, s, N_BRANCH, D_MODEL)
        mixed = gates[:, :, 0] * y_attn + gates[:, :, 1] * y_dn
        x = x + (mixed @ w_out[l]).astype(x.dtype)

        h = rmsnorm(x, norm_ffn_g[l])
        u = centred_depthwise_conv(h @ w_up[l], ffn_conv_w[l])
        u_gate, u_val = jnp.split(u, 2, axis=-1)
        x = x + ((jax.nn.silu(u_gate) * u_val) @ w_down[l]).astype(x.dtype)
    return x
```

```python
import functools
import math

import jax
import jax.numpy as jnp
import numpy as np
from jax import lax
from jax.experimental import pallas as pl
from jax.experimental.pallas import tpu as pltpu

D_MODEL = 1024
DEPTH = 4
GRID_W = 64
EPS = 1e-6

A_HEADS = 8
A_KV_HEADS = 2
A_HEAD_DIM = 64
A_GROUP = A_HEADS // A_KV_HEADS
ROPE_THETA = 10000.0
ROPE_FREQS = A_HEAD_DIM // 4

DN_HEADS = 8
DN_DK = 64
DN_DV = 64
DN_CONV_W = 5
DN_CHUNK = 64

D_FF = 2816
FFN_CONV_W = 3

A_Q = A_HEADS * A_HEAD_DIM
A_KV = A_KV_HEADS * A_HEAD_DIM
DN_QK = DN_HEADS * DN_DK
DN_V = DN_HEADS * DN_DV

LANES = 128
BF16_SUBLANES = 16
F32_SUBLANES = 8
VMEM_LIMIT = 56 * 1024 * 1024

W_A = A_Q + 2 * A_KV
W_D = 2 * DN_QK + DN_V
W_Z = DN_V
W_G = 2 * D_MODEL
W_BG = LANES
HEADS_PER_STEP = 2
FF_CHUNK = 256

LOG2E = 1.4426950408889634
NEG_BIG = -1e30

bf16 = jnp.bfloat16
f32 = jnp.float32


def _params(sem):
    return pltpu.CompilerParams(dimension_semantics=sem, vmem_limit_bytes=VMEM_LIMIT)


def _resident(shape):
    nd = len(shape)
    return pl.BlockSpec(shape, lambda *_: (0,) * nd, pipeline_mode=pl.Buffered(1))


def _dot(a, b):
    return jnp.dot(a.astype(bf16), b.astype(bf16), preferred_element_type=f32)


def _dot_nt(a, b):
    return lax.dot_general(a.astype(bf16), b.astype(bf16), (((1,), (1,)), ((), ())),
                           preferred_element_type=f32)


def _dot_tn(a, b):
    return lax.dot_general(a.astype(bf16), b.astype(bf16), (((0,), (0,)), ((), ())),
                           preferred_element_type=f32)


def _sigmoid(x):
    return 1.0 / (1.0 + jnp.exp(-x))


def _split3(x):
    hi = x.astype(bf16)
    r1 = x - hi.astype(f32)
    mid = r1.astype(bf16)
    lo = (r1 - mid.astype(f32)).astype(bf16)
    return hi, mid, lo


def _group_sum(x, width):
    r = lax.broadcasted_iota(jnp.int32, (LANES, LANES), 0) // width
    c = lax.broadcasted_iota(jnp.int32, (LANES, LANES), 1) // width
    ones_bd = jnp.where(r == c, 1.0, 0.0).astype(bf16)
    hi, mid, lo = _split3(x)
    return (jnp.dot(hi, ones_bd, preferred_element_type=f32)
            + jnp.dot(mid, ones_bd, preferred_element_type=f32)
            + jnp.dot(lo, ones_bd, preferred_element_type=f32))


def _inproj_kernel(x_ref, g_ref, w_ref, oa_ref, od_ref, oz_ref, og_ref, obg_ref):
    x = x_ref[...]
    ms = jnp.mean(x * x, axis=-1, keepdims=True)
    h = (x * lax.rsqrt(ms + EPS) * g_ref[...]).astype(bf16)
    c = 0
    for o_ref in (oa_ref, od_ref, oz_ref, og_ref, obg_ref):
        n = o_ref.shape[-1]
        o_ref[...] = jnp.dot(h, w_ref[:, c:c + n], preferred_element_type=f32).astype(o_ref.dtype)
        c += n


def _inproj(x2, g, w_cat, tm):
    t = x2.shape[0]
    n_cat = w_cat.shape[1]
    widths = (W_A, W_D, W_Z, W_G, W_BG)
    dtypes = (bf16, bf16, bf16, bf16, f32)
    return pl.pallas_call(
        _inproj_kernel,
        grid=(t // tm,),
        in_specs=[pl.BlockSpec((tm, D_MODEL), lambda i: (i, 0)),
                  _resident((1, D_MODEL)),
                  _resident((D_MODEL, n_cat))],
        out_specs=[pl.BlockSpec((tm, w), lambda i: (i, 0)) for w in widths],
        out_shape=[jax.ShapeDtypeStruct((t, w), d) for w, d in zip(widths, dtypes)],
        compiler_params=_params(("parallel",)),
        name="inproj",
    )(x2, g, w_cat)


def _swap16(x):
    lane = lax.broadcasted_iota(jnp.int32, x.shape, 1)
    return jnp.where((lane % 32) < 16, pltpu.roll(x, LANES - 16, 1), pltpu.roll(x, 16, 1))


def _prep_kernel(a_ref, d_ref, dprev_ref, dnext_ref, bg_ref, qg_ref, kg_ref, cos_ref, sin_ref,
                 cw_ref, alog_ref, bias_ref,
                 q_ref, k_ref, v_ref, dq_ref, dk_ref, dv_ref, gates_ref,
                 xs_ref, *, tm, n_tiles):
    i = pl.program_id(1)
    cos = cos_ref[...]
    sin = sin_ref[...]
    lane = lax.broadcasted_iota(jnp.int32, (tm, LANES), 1)

    def norm_rope(x, g):
        ms = _group_sum(x * x, A_HEAD_DIM) * (1.0 / A_HEAD_DIM)
        y = x * lax.rsqrt(ms + EPS) * g
        return y * cos + _swap16(y) * sin

    qscale = (A_HEAD_DIM ** -0.5) * LOG2E
    for j in range(A_Q // LANES):
        y = norm_rope(a_ref[0, :, j * LANES:(j + 1) * LANES].astype(f32), qg_ref[...]) * qscale
        q_ref[0, 2 * j] = y[:, :A_HEAD_DIM].astype(bf16)
        q_ref[0, 2 * j + 1] = y[:, A_HEAD_DIM:].astype(bf16)
    y = norm_rope(a_ref[0, :, A_Q:A_Q + LANES].astype(f32), kg_ref[...])
    k_ref[0, 0] = y[:, :A_HEAD_DIM].astype(bf16)
    k_ref[0, 1] = y[:, A_HEAD_DIM:].astype(bf16)
    v = a_ref[0, :, A_Q + A_KV:A_Q + 2 * A_KV].astype(f32)
    ones_col = jnp.where(lane == A_HEAD_DIM, 1.0, 0.0)
    v_ref[0, 0] = jnp.where(lane < A_HEAD_DIM, v, ones_col).astype(bf16)
    v_ref[0, 1] = jnp.where(lane < A_HEAD_DIM, pltpu.roll(v, A_HEAD_DIM, 1), ones_col).astype(bf16)

    halo = BF16_SUBLANES
    xs_ref[0:halo, :] = jnp.where(i > 0, dprev_ref[0].astype(f32), 0.0)
    xs_ref[halo:halo + tm, :] = d_ref[0].astype(f32)
    xs_ref[halo + tm:, :] = jnp.where(i < n_tiles - 1, dnext_ref[0].astype(f32), 0.0)
    pad = DN_CONV_W // 2
    for j in range(W_D // LANES):
        cols = slice(j * LANES, (j + 1) * LANES)
        acc = None
        for kk in range(DN_CONV_W):
            term = xs_ref[pl.ds(halo - pad + kk, tm), cols] * cw_ref[kk:kk + 1, cols]
            acc = term if acc is None else acc + term
        y = acc * _sigmoid(acc)
        if j < 2 * DN_QK // LANES:
            y = y * lax.rsqrt(_group_sum(y * y, DN_DK) + EPS)
        if j < DN_QK // LANES:
            dq_ref[0, :, cols] = (y * (DN_DK ** -0.5)).astype(bf16)
        elif j < 2 * DN_QK // LANES:
            jj = j - DN_QK // LANES
            dk_ref[0, :, jj * LANES:(jj + 1) * LANES] = y.astype(bf16)
        else:
            jj = j - 2 * DN_QK // LANES
            dv_ref[0, :, jj * LANES:(jj + 1) * LANES] = y.astype(bf16)

    bg = bg_ref[0]
    beta = _sigmoid(bg)
    xb = bg + bias_ref[...]
    softplus = jnp.maximum(xb, 0.0) + jnp.log1p(jnp.exp(-jnp.abs(xb)))
    g = -jnp.exp(alog_ref[...]) * softplus
    r = lax.broadcasted_iota(jnp.int32, (tm, tm), 0)
    c = lax.broadcasted_iota(jnp.int32, (tm, tm), 1)
    same = (r // DN_CHUNK) == (c // DN_CHUNK)
    lower = jnp.where(same & (r >= c), 1.0, 0.0).astype(bf16)
    upper = jnp.where(same & (r <= c), 1.0, 0.0).astype(bf16)
    pieces = _split3(g)
    pref = sum(jnp.dot(lower, p, preferred_element_type=f32) for p in pieces)
    suff = sum(jnp.dot(upper, p, preferred_element_type=f32) for p in pieces)
    gc = jnp.where(lane < 2 * DN_HEADS + DN_HEADS, pref, suff)
    gates_ref[0] = jnp.where(lane < 2 * DN_HEADS, beta, gc)


def _prep(a, d, bg, qg, kg, cos_t, sin_t, cw, alog, bias, tm):
    b, s, _ = a.shape
    n_tiles = s // tm
    halo = BF16_SUBLANES
    per = tm // halo
    last = s // halo - 1
    kern = functools.partial(_prep_kernel, tm=tm, n_tiles=n_tiles)
    return pl.pallas_call(
        kern,
        grid=(b, n_tiles),
        in_specs=[
            pl.BlockSpec((1, tm, W_A), lambda bi, i: (bi, i, 0)),
            pl.BlockSpec((1, tm, W_D), lambda bi, i: (bi, i, 0)),
            pl.BlockSpec((1, halo, W_D), lambda bi, i: (bi, jnp.maximum(i * per - 1, 0), 0)),
            pl.BlockSpec((1, halo, W_D), lambda bi, i: (bi, jnp.minimum((i + 1) * per, last), 0)),
            pl.BlockSpec((1, tm, W_BG), lambda bi, i: (bi, i, 0)),
            _resident((1, LANES)),
            _resident((1, LANES)),
            pl.BlockSpec((tm, LANES), lambda bi, i: (i, 0)),
            pl.BlockSpec((tm, LANES), lambda bi, i: (i, 0)),
            _resident((F32_SUBLANES, W_D)),
            _resident((1, LANES)),
            _resident((1, LANES)),
        ],
        out_specs=[
            pl.BlockSpec((1, A_HEADS, tm, A_HEAD_DIM), lambda bi, i: (bi, 0, i, 0)),
            pl.BlockSpec((1, A_KV_HEADS, tm, A_HEAD_DIM), lambda bi, i: (bi, 0, i, 0)),
            pl.BlockSpec((1, A_KV_HEADS, tm, LANES), lambda bi, i: (bi, 0, i, 0)),
            pl.BlockSpec((1, tm, DN_QK), lambda bi, i: (bi, i, 0)),
            pl.BlockSpec((1, tm, DN_QK), lambda bi, i: (bi, i, 0)),
            pl.BlockSpec((1, tm, DN_V), lambda bi, i: (bi, i, 0)),
            pl.BlockSpec((1, tm, LANES), lambda bi, i: (bi, i, 0)),
        ],
        out_shape=[
            jax.ShapeDtypeStruct((b, A_HEADS, s, A_HEAD_DIM), bf16),
            jax.ShapeDtypeStruct((b, A_KV_HEADS, s, A_HEAD_DIM), bf16),
            jax.ShapeDtypeStruct((b, A_KV_HEADS, s, LANES), bf16),
            jax.ShapeDtypeStruct((b, s, DN_QK), bf16),
            jax.ShapeDtypeStruct((b, s, DN_QK), bf16),
            jax.ShapeDtypeStruct((b, s, DN_V), bf16),
            jax.ShapeDtypeStruct((b, s, LANES), f32),
        ],
        scratch_shapes=[pltpu.VMEM((tm + 2 * halo, W_D), f32)],
        compiler_params=_params(("parallel", "parallel")),
        name="prep",
    )(a, d, d, d, bg, qg, kg, cos_t, sin_t, cw, alog, bias)


def _attn_kernel(q_ref, k_ref, v_ref, o_ref, m_ref, acc_ref, *, tq, tk, n_kv):
    m_rows = A_GROUP * tq
    q = q_ref[0].reshape(m_rows, A_HEAD_DIM)
    m_ref[...] = jnp.full(m_ref.shape, NEG_BIG, f32)
    acc_ref[...] = jnp.zeros(acc_ref.shape, f32)

    def kv_step(j, carry):
        rows = pl.ds(pl.multiple_of(j * tk, tk), tk)
        s = lax.dot_general(q, k_ref[0, 0, rows, :], (((1,), (1,)), ((), ())),
                            preferred_element_type=f32)
        m_old = m_ref[...]
        m_new = jnp.maximum(m_old, jnp.max(s, axis=-1, keepdims=True))
        p = jnp.exp2(s - m_new).astype(bf16)
        acc_ref[...] = acc_ref[...] * jnp.exp2(m_old - m_new) + jnp.dot(
            p, v_ref[0, 0, rows, :], preferred_element_type=f32)
        m_ref[...] = m_new
        return carry

    lax.fori_loop(0, n_kv, kv_step, 0)
    acc = acc_ref[...]
    o = acc[:, :A_HEAD_DIM] * (1.0 / acc[:, A_HEAD_DIM:A_HEAD_DIM + 1])
    for g in range(A_GROUP):
        o_ref[0, :, g * A_HEAD_DIM:(g + 1) * A_HEAD_DIM] = o[g * tq:(g + 1) * tq].astype(o_ref.dtype)


def _attention(q, k, v, tq, tk):
    b, _, s, _ = q.shape
    kern = functools.partial(_attn_kernel, tq=tq, tk=tk, n_kv=s // tk)
    m_rows = A_GROUP * tq
    return pl.pallas_call(
        kern,
        grid=(b, A_KV_HEADS, s // tq),
        in_specs=[
            pl.BlockSpec((1, A_GROUP, tq, A_HEAD_DIM), lambda bi, h, i: (bi, h, i, 0)),
            pl.BlockSpec((1, 1, s, A_HEAD_DIM), lambda bi, h, i: (bi, h, 0, 0)),
            pl.BlockSpec((1, 1, s, LANES), lambda bi, h, i: (bi, h, 0, 0)),
        ],
        out_specs=pl.BlockSpec((1, tq, A_GROUP * A_HEAD_DIM), lambda bi, h, i: (bi, i, h)),
        out_shape=jax.ShapeDtypeStruct((b, s, A_Q), bf16),
        scratch_shapes=[pltpu.VMEM((m_rows, 1), f32), pltpu.VMEM((m_rows, LANES), f32)],
        compiler_params=_params(("parallel", "parallel", "arbitrary")),
        name="attn",
    )(q, k, v)


def _tri_inverse(a, bd16, eye):
    d = jnp.where(bd16, a, 0.0)
    o = a - d
    d2 = _dot(d, d)
    d4 = _dot(d2, d2)
    d8 = _dot(d4, d4)
    t0 = eye - d
    t0 = t0 + _dot(t0, d2)
    t0 = t0 + _dot(t0, d4)
    t0 = t0 + _dot(t0, d8)
    m = _dot(t0, o)
    m2 = _dot(m, m)
    t1 = t0 + _dot(m2, t0)
    return t1 - _dot(m, t1)


def _deltanet_kernel(q_ref, k_ref, v_ref, gcol_ref, grow_ref, o_ref, s_ref, *, n_chunks):
    c = DN_CHUNK
    hp = HEADS_PER_STEP
    lane = lax.broadcasted_iota(jnp.int32, (c, LANES), 1)
    head0 = lane < DN_DK
    r64 = lax.broadcasted_iota(jnp.int32, (c, c), 0)
    c64 = lax.broadcasted_iota(jnp.int32, (c, c), 1)
    eye = jnp.where(r64 == c64, 1.0, 0.0)
    bd16 = (r64 // 16) == (c64 // 16)
    rs = lax.broadcasted_iota(jnp.int32, (LANES, LANES), 0) // DN_DK
    cs = lax.broadcasted_iota(jnp.int32, (LANES, LANES), 1) // DN_DV
    state_bd = rs == cs

    o_ref[...] = jnp.zeros(o_ref.shape, f32)
    s_ref[...] = jnp.zeros(s_ref.shape, f32)

    def pack(col0, col1):
        return jnp.where(head0, col0, col1)

    def chunk_step(direction, ci):
        rows = pl.ds(pl.multiple_of(ci * c, c), c)
        q2 = q_ref[0, rows, :]
        k2 = k_ref[0, rows, :]
        v2 = v_ref[0, rows, :].astype(f32)
        gcol = gcol_ref[0, 0, rows, :]
        grow = grow_ref[0, 0, ci]
        o = direction * hp
        if direction == 0:
            incl, strict, tot_row = r64 >= c64, r64 > c64, c - 1
        else:
            incl, strict, tot_row = r64 <= c64, r64 < c64, 0

        beta2 = pack(gcol[:, o:o + 1], gcol[:, o + 1:o + 2])
        gc2 = pack(gcol[:, 2 * hp + o:2 * hp + o + 1], gcol[:, 2 * hp + o + 1:2 * hp + o + 2])
        gt2 = gc2[tot_row:tot_row + 1, :]
        e_gc2 = jnp.exp(gc2)
        kf = k2.astype(f32)
        kb2 = kf * beta2
        x_v = v2 * beta2
        x_w = kb2 * e_gc2
        qdec2 = q2.astype(f32) * e_gc2
        kdec2 = kf * jnp.exp(gt2 - gc2)

        zero = jnp.zeros_like(q2)
        stack = jnp.concatenate([jnp.where(head0, q2, zero), jnp.where(head0, zero, q2),
                                 jnp.where(head0, k2, zero), jnp.where(head0, zero, k2)], axis=0)
        prod = _dot_nt(stack, k2)
        x_vw = jnp.concatenate([x_v, x_w], axis=1).astype(bf16)

        uw = []
        qkd = []
        for h in range(hp):
            gcc = gcol[:, 2 * hp + o + h:2 * hp + o + h + 1]
            gcr = grow[o + h:o + h + 1, :]
            decay = jnp.exp(jnp.where(incl, gcc - gcr, -jnp.inf))
            kk = prod[(hp + h) * c:(hp + h + 1) * c]
            a = jnp.where(strict, gcol[:, o + h:o + h + 1] * kk * decay, 0.0)
            t_inv = _tri_inverse(a, bd16, eye)
            uw.append(jnp.dot(t_inv.astype(bf16), x_vw, preferred_element_type=f32))
            qkd.append((prod[h * c:(h + 1) * c] * decay).astype(bf16))
        u2 = pack(uw[0][:, :LANES], uw[1][:, :LANES])
        w2 = pack(uw[0][:, LANES:], uw[1][:, LANES:])

        state = s_ref[direction]
        ws = _dot(jnp.concatenate([w2, qdec2], axis=0), state)
        v_new = (u2 - ws[:c]).astype(bf16)
        o2 = ws[c:] + pack(jnp.dot(qkd[0], v_new, preferred_element_type=f32),
                           jnp.dot(qkd[1], v_new, preferred_element_type=f32))
        upd = _dot_tn(kdec2, v_new)
        s_ref[direction] = jnp.where(state_bd, state * jnp.exp(gt2) + upd, 0.0)
        o_ref[0, rows, :] = o_ref[0, rows, :] + o2

    def body(i, carry):
        chunk_step(0, i)
        chunk_step(1, n_chunks - 1 - i)
        return carry

    lax.fori_loop(0, n_chunks, body, 0)


def _deltanet(dq, dk, dv, gcol, grow):
    b, s, _ = dq.shape
    n_chunks = s // DN_CHUNK
    n_pairs = DN_HEADS // HEADS_PER_STEP
    kern = functools.partial(_deltanet_kernel, n_chunks=n_chunks)
    spec = pl.BlockSpec((1, s, LANES), lambda bi, p: (bi, 0, p))
    return pl.pallas_call(
        kern,
        grid=(b, n_pairs),
        in_specs=[spec, spec, spec,
                  pl.BlockSpec((1, 1, s, 4 * HEADS_PER_STEP), lambda bi, p: (bi, p, 0, 0)),
                  pl.BlockSpec((1, 1, n_chunks, 2 * HEADS_PER_STEP, DN_CHUNK),
                               lambda bi, p: (bi, p, 0, 0, 0))],
        out_specs=spec,
        out_shape=jax.ShapeDtypeStruct((b, s, DN_V), f32),
        scratch_shapes=[pltpu.VMEM((2, LANES, LANES), f32)],
        compiler_params=_params(("parallel", "parallel")),
        name="deltanet",
    )(dq, dk, dv, gcol, grow)


def _merge_kernel(oa_ref, od_ref, z_ref, gl_ref, x_ref, dng_ref, woa_ref, wod_ref, wout_ref, out_ref):
    ya = jnp.dot(oa_ref[...], woa_ref[...], preferred_element_type=f32)
    parts = []
    for j in range(DN_V // LANES):
        cols = slice(j * LANES, (j + 1) * LANES)
        o = od_ref[:, cols]
        ms = _group_sum(o * o, DN_DV) * (1.0 / DN_DV)
        z = z_ref[:, cols].astype(f32)
        parts.append((o * lax.rsqrt(ms + EPS) * dng_ref[...] * (z * _sigmoid(z))).astype(bf16))
    yd = jnp.dot(jnp.concatenate(parts, axis=1), wod_ref[...], preferred_element_type=f32)
    g0 = _sigmoid(gl_ref[:, :D_MODEL].astype(f32))
    g1 = _sigmoid(gl_ref[:, D_MODEL:].astype(f32))
    mixed = (g0 * ya + g1 * yd).astype(bf16)
    out_ref[...] = x_ref[...] + jnp.dot(mixed, wout_ref[...], preferred_element_type=f32)


def _merge(oa, od, z, gl, x2, dng, woa, wod, wout, tm):
    t = x2.shape[0]
    row = lambda w: pl.BlockSpec((tm, w), lambda i: (i, 0))
    return pl.pallas_call(
        _merge_kernel,
        grid=(t // tm,),
        in_specs=[row(A_Q), row(DN_V), row(W_Z), row(W_G), row(D_MODEL),
                  _resident((1, LANES)), _resident((A_Q, D_MODEL)), _resident((DN_V, D_MODEL)),
                  _resident((D_MODEL, D_MODEL))],
        out_specs=row(D_MODEL),
        out_shape=jax.ShapeDtypeStruct((t, D_MODEL), f32),
        compiler_params=_params(("parallel",)),
        name="merge",
    )(oa, od, z, gl, x2, dng, woa, wod, wout)


def _ffn_kernel(x_ref, xprev_ref, xnext_ref, g_ref, wg_ref, wv_ref, cg_ref, cv_ref, wd_ref, out_ref,
                h_ref, ug_ref, uv_ref, *, tm, n_tiles):
    i = pl.program_id(1)
    halo = F32_SUBLANES

    def norm(x):
        ms = jnp.mean(x * x, axis=-1, keepdims=True)
        return x * lax.rsqrt(ms + EPS) * g_ref[...]

    h_ref[0:halo, :] = jnp.where(i > 0, norm(xprev_ref[0]), 0.0).astype(bf16)
    h_ref[halo:halo + tm, :] = norm(x_ref[0]).astype(bf16)
    h_ref[halo + tm:, :] = jnp.where(i < n_tiles - 1, norm(xnext_ref[0]), 0.0).astype(bf16)
    h = h_ref[...]

    def conv(u_ref, cw):
        acc = None
        for kk in range(FFN_CONV_W):
            term = u_ref[pl.ds(halo - FFN_CONV_W // 2 + kk, tm), :] * cw[kk:kk + 1, :]
            acc = term if acc is None else acc + term
        return acc

    acc = x_ref[0]
    for j in range(D_FF // FF_CHUNK):
        ug_ref[...] = jnp.dot(h, wg_ref[j], preferred_element_type=f32)
        uv_ref[...] = jnp.dot(h, wv_ref[j], preferred_element_type=f32)
        gate = conv(ug_ref, cg_ref[j])
        val = conv(uv_ref, cv_ref[j])
        act = (gate * _sigmoid(gate) * val).astype(bf16)
        acc = acc + jnp.dot(act, wd_ref[j], preferred_element_type=f32)
    out_ref[0] = acc


def _ffn(x3, g, wg, wv, cg, cv, wd, tm):
    b, s, _ = x3.shape
    n_tiles = s // tm
    halo = F32_SUBLANES
    per = tm // halo
    last = s // halo - 1
    n_ff = D_FF // FF_CHUNK
    kern = functools.partial(_ffn_kernel, tm=tm, n_tiles=n_tiles)
    return pl.pallas_call(
        kern,
        grid=(b, n_tiles),
        in_specs=[
            pl.BlockSpec((1, tm, D_MODEL), lambda bi, i: (bi, i, 0)),
            pl.BlockSpec((1, halo, D_MODEL), lambda bi, i: (bi, jnp.maximum(i * per - 1, 0), 0)),
            pl.BlockSpec((1, halo, D_MODEL), lambda bi, i: (bi, jnp.minimum((i + 1) * per, last), 0)),
            _resident((1, D_MODEL)),
            _resident((n_ff, D_MODEL, FF_CHUNK)),
            _resident((n_ff, D_MODEL, FF_CHUNK)),
            _resident((n_ff, F32_SUBLANES, FF_CHUNK)),
            _resident((n_ff, F32_SUBLANES, FF_CHUNK)),
            _resident((n_ff, FF_CHUNK, D_MODEL)),
        ],
        out_specs=pl.BlockSpec((1, tm, D_MODEL), lambda bi, i: (bi, i, 0)),
        out_shape=jax.ShapeDtypeStruct((b, s, D_MODEL), f32),
        scratch_shapes=[pltpu.VMEM((tm + 2 * halo, D_MODEL), bf16),
                        pltpu.VMEM((tm + 2 * halo, FF_CHUNK), f32),
                        pltpu.VMEM((tm + 2 * halo, FF_CHUNK), f32)],
        compiler_params=_params(("parallel", "parallel")),
        name="ffn",
    )(x3, x3, x3, g, wg, wv, cg, cv, wd)


def _rope_tables(s):
    rows_n = s // GRID_W
    row = jnp.repeat(jnp.arange(rows_n), GRID_W).astype(f32)
    col = jnp.tile(jnp.arange(GRID_W), rows_n).astype(f32)
    inv_freq = ROPE_THETA ** (-jnp.arange(ROPE_FREQS, dtype=f32) / ROPE_FREQS)
    ang_r = row[:, None] * inv_freq
    ang_c = col[:, None] * inv_freq
    cos64 = jnp.concatenate([jnp.cos(ang_r), jnp.cos(ang_r), jnp.cos(ang_c), jnp.cos(ang_c)], axis=1)
    sin64 = jnp.concatenate([-jnp.sin(ang_r), jnp.sin(ang_r), -jnp.sin(ang_c), jnp.sin(ang_c)], axis=1)
    return jnp.tile(cos64, (1, 2)), jnp.tile(sin64, (1, 2))


def _pad_rows(w, rows):
    return jnp.pad(w, ((0, rows - w.shape[0]), (0, 0)))


def _lane_row(vals, offset):
    return jnp.pad(vals.astype(f32), (offset, LANES - offset - vals.shape[0]))[None, :]


def kernel(x, norm_mix_g, w_in, q_norm_g, k_norm_g, dn_conv_w, dn_a_log, dn_dt_bias, dn_out_norm_g,
           w_o_attn, w_o_dn, w_out, norm_ffn_g, w_up, ffn_conv_w, w_down):
    b, s, d = x.shape
    assert d == D_MODEL and s % 512 == 0 and s % GRID_W == 0
    t = b * s
    tm = 512
    n_chunks = s // DN_CHUNK
    n_pairs = DN_HEADS // HEADS_PER_STEP
    hp = HEADS_PER_STEP
    cos_t, sin_t = _rope_tables(s)
    offs = np.cumsum((A_Q, A_KV, A_KV, DN_QK, DN_QK, DN_V, 2 * DN_HEADS, 2 * DN_HEADS, DN_V))
    n_ff = D_FF // FF_CHUNK

    for l in range(DEPTH):
        wl = w_in[l]
        w_cat = jnp.concatenate(
            [wl[:, :offs[5]], wl[:, offs[7]:offs[8]], wl[:, offs[8]:], wl[:, offs[5]:offs[7]],
             jnp.zeros((D_MODEL, LANES - 4 * DN_HEADS), wl.dtype)], axis=1).astype(bf16)
        a, dd, z, gl, bg = _inproj(x.reshape(t, d), norm_mix_g[l][None, :], w_cat, tm)

        q, k, v, dq, dk, dv, gates = _prep(
            a.reshape(b, s, W_A), dd.reshape(b, s, W_D), bg.reshape(b, s, W_BG),
            jnp.tile(q_norm_g[l], 2)[None, :], jnp.tile(k_norm_g[l], 2)[None, :], cos_t, sin_t,
            _pad_rows(dn_conv_w[l], F32_SUBLANES),
            _lane_row(dn_a_log[l].reshape(-1), 2 * DN_HEADS),
            _lane_row(dn_dt_bias[l].reshape(-1), 2 * DN_HEADS), tm)

        o_attn = _attention(q, k, v, 256, 512)

        g4 = gates[:, :, :4 * DN_HEADS].reshape(b, s, 4, n_pairs, hp)
        gcol = jnp.transpose(g4, (0, 3, 1, 2, 4)).reshape(b, n_pairs, s, 4 * hp)
        grow = jnp.transpose(g4[:, :, 2:].reshape(b, n_chunks, DN_CHUNK, 2, n_pairs, hp),
                             (0, 4, 1, 3, 5, 2)).reshape(b, n_pairs, n_chunks, 2 * hp, DN_CHUNK)
        o_dn = _deltanet(dq, dk, dv, gcol, grow)

        x = _merge(o_attn.reshape(t, A_Q), o_dn.reshape(t, DN_V), z, gl, x.reshape(t, d),
                   jnp.tile(dn_out_norm_g[l], 2)[None, :], w_o_attn[l].astype(bf16),
                   w_o_dn[l].astype(bf16), w_out[l].astype(bf16), tm).reshape(b, s, d)

        wu = w_up[l].astype(bf16)
        wg = jnp.transpose(wu[:, :D_FF].reshape(d, n_ff, FF_CHUNK), (1, 0, 2))
        wv = jnp.transpose(wu[:, D_FF:].reshape(d, n_ff, FF_CHUNK), (1, 0, 2))
        cwp = _pad_rows(ffn_conv_w[l], F32_SUBLANES)
        cg = jnp.transpose(cwp[:, :D_FF].reshape(F32_SUBLANES, n_ff, FF_CHUNK), (1, 0, 2))
        cv = jnp.transpose(cwp[:, D_FF:].reshape(F32_SUBLANES, n_ff, FF_CHUNK), (1, 0, 2))
        wd = w_down[l].astype(bf16).reshape(n_ff, FF_CHUNK, d)
        x = _ffn(x, norm_ffn_g[l][None, :], wg, wv, cg, cv, wd, tm)
    return x
```

```python
import functools
import math

import jax
import jax.numpy as jnp
import numpy as np
from jax import lax
from jax.experimental import pallas as pl
from jax.experimental.pallas import tpu as pltpu

D_MODEL = 1024
DEPTH = 4
GRID_W = 64
EPS = 1e-6

A_HEADS = 8
A_KV_HEADS = 2
A_HEAD_DIM = 64
A_GROUP = A_HEADS // A_KV_HEADS
ROPE_THETA = 10000.0
ROPE_FREQS = A_HEAD_DIM // 4

DN_HEADS = 8
DN_DK = 64
DN_DV = 64
DN_CONV_W = 5
DN_CHUNK = 64

D_FF = 2816
FFN_CONV_W = 3

A_Q = A_HEADS * A_HEAD_DIM
A_KV = A_KV_HEADS * A_HEAD_DIM
DN_QK = DN_HEADS * DN_DK
DN_V = DN_HEADS * DN_DV

LANES = 128
BF16_SUBLANES = 16
F32_SUBLANES = 8
VMEM_LIMIT = 56 * 1024 * 1024

W_A = A_Q + 2 * A_KV
W_D = 2 * DN_QK + DN_V
W_Z = DN_V
W_G = 2 * D_MODEL
W_BG = LANES
HEADS_PER_STEP = 2
FF_CHUNK = 256

LOG2E = 1.4426950408889634
NEG_BIG = -1e30

bf16 = jnp.bfloat16
f32 = jnp.float32


def _params(sem):
    return pltpu.CompilerParams(dimension_semantics=sem, vmem_limit_bytes=VMEM_LIMIT)


def _resident(shape):
    nd = len(shape)
    return pl.BlockSpec(shape, lambda *_: (0,) * nd, pipeline_mode=pl.Buffered(1))


def _dot(a, b):
    return jnp.dot(a.astype(bf16), b.astype(bf16), preferred_element_type=f32)


def _dot_nt(a, b):
    return lax.dot_general(a.astype(bf16), b.astype(bf16), (((1,), (1,)), ((), ())),
                           preferred_element_type=f32)


def _dot_tn(a, b):
    return lax.dot_general(a.astype(bf16), b.astype(bf16), (((0,), (0,)), ((), ())),
                           preferred_element_type=f32)


def _sigmoid(x):
    return 1.0 / (1.0 + jnp.exp(-x))


def _split3(x):
    hi = x.astype(bf16)
    r1 = x - hi.astype(f32)
    mid = r1.astype(bf16)
    lo = (r1 - mid.astype(f32)).astype(bf16)
    return hi, mid, lo


def _group_sum(x, width):
    r = lax.broadcasted_iota(jnp.int32, (LANES, LANES), 0) // width
    c = lax.broadcasted_iota(jnp.int32, (LANES, LANES), 1) // width
    ones_bd = jnp.where(r == c, 1.0, 0.0).astype(bf16)
    hi, mid, lo = _split3(x)
    return (jnp.dot(hi, ones_bd, preferred_element_type=f32)
            + jnp.dot(mid, ones_bd, preferred_element_type=f32)
            + jnp.dot(lo, ones_bd, preferred_element_type=f32))


def _inproj_kernel(x_ref, g_ref, w_ref, oa_ref, od_ref, oz_ref, og_ref, obg_ref):
    x = x_ref[...]
    ms = jnp.mean(x * x, axis=-1, keepdims=True)
    h = (x * lax.rsqrt(ms + EPS) * g_ref[...]).astype(bf16)
    c = 0
    for o_ref in (oa_ref, od_ref, oz_ref, og_ref, obg_ref):
        n = o_ref.shape[-1]
        o_ref[...] = jnp.dot(h, w_ref[:, c:c + n], preferred_element_type=f32).astype(o_ref.dtype)
        c += n


def _inproj(x2, g, w_cat, tm):
    t = x2.shape[0]
    n_cat = w_cat.shape[1]
    widths = (W_A, W_D, W_Z, W_G, W_BG)
    dtypes = (bf16, bf16, bf16, bf16, f32)
    return pl.pallas_call(
        _inproj_kernel,
        grid=(t // tm,),
        in_specs=[pl.BlockSpec((tm, D_MODEL), lambda i: (i, 0)),
                  _resident((1, D_MODEL)),
                  _resident((D_MODEL, n_cat))],
        out_specs=[pl.BlockSpec((tm, w), lambda i: (i, 0)) for w in widths],
        out_shape=[jax.ShapeDtypeStruct((t, w), d) for w, d in zip(widths, dtypes)],
        compiler_params=_params(("parallel",)),
        name="inproj",
    )(x2, g, w_cat)


def _swap16(x):
    lane = lax.broadcasted_iota(jnp.int32, x.shape, 1)
    return jnp.where((lane % 32) < 16, pltpu.roll(x, LANES - 16, 1), pltpu.roll(x, 16, 1))


def _prep_kernel(a_ref, d_ref, dprev_ref, dnext_ref, bg_ref, qg_ref, kg_ref, cos_ref, sin_ref,
                 cw_ref, alog_ref, bias_ref,
                 q_ref, k_ref, vt_ref, dq_ref, dk_ref, dv_ref, gates_ref,
                 xs_ref, *, tm, n_tiles):
    i = pl.program_id(1)
    cos = cos_ref[...]
    sin = sin_ref[...]
    lane = lax.broadcasted_iota(jnp.int32, (tm, LANES), 1)

    def norm_rope(x, g):
        ms = _group_sum(x * x, A_HEAD_DIM) * (1.0 / A_HEAD_DIM)
        y = x * lax.rsqrt(ms + EPS) * g
        return y * cos + _swap16(y) * sin

    qscale = (A_HEAD_DIM ** -0.5) * LOG2E
    for j in range(A_Q // LANES):
        y = norm_rope(a_ref[0, :, j * LANES:(j + 1) * LANES].astype(f32), qg_ref[...]) * qscale
        q_ref[0, 2 * j] = y[:, :A_HEAD_DIM].astype(bf16)
        q_ref[0, 2 * j + 1] = y[:, A_HEAD_DIM:].astype(bf16)
    y = norm_rope(a_ref[0, :, A_Q:A_Q + LANES].astype(f32), kg_ref[...])
    k_ref[0, 0] = y[:, :A_HEAD_DIM].astype(bf16)
    k_ref[0, 1] = y[:, A_HEAD_DIM:].astype(bf16)
    v = a_ref[0, :, A_Q + A_KV:A_Q + 2 * A_KV].astype(f32)
    ones_col = jnp.where(lane == A_HEAD_DIM, 1.0, 0.0)
    vt_ref[0, 0, 0] = jnp.where(lane < A_HEAD_DIM, v, ones_col).T.astype(bf16)
    vt_ref[0, 1, 0] = jnp.where(lane < A_HEAD_DIM, pltpu.roll(v, A_HEAD_DIM, 1), ones_col).T.astype(bf16)

    halo = BF16_SUBLANES
    xs_ref[0:halo, :] = jnp.where(i > 0, dprev_ref[0].astype(f32), 0.0)
    xs_ref[halo:halo + tm, :] = d_ref[0].astype(f32)
    xs_ref[halo + tm:, :] = jnp.where(i < n_tiles - 1, dnext_ref[0].astype(f32), 0.0)
    pad = DN_CONV_W // 2
    for j in range(W_D // LANES):
        cols = slice(j * LANES, (j + 1) * LANES)
        acc = None
        for kk in range(DN_CONV_W):
            term = xs_ref[pl.ds(halo - pad + kk, tm), cols] * cw_ref[kk:kk + 1, cols]
            acc = term if acc is None else acc + term
        y = acc * _sigmoid(acc)
        if j < 2 * DN_QK // LANES:
            y = y * lax.rsqrt(_group_sum(y * y, DN_DK) + EPS)
        if j < DN_QK // LANES:
            dq_ref[0, :, cols] = (y * (DN_DK ** -0.5)).astype(bf16)
        elif j < 2 * DN_QK // LANES:
            jj = j - DN_QK // LANES
            dk_ref[0, :, jj * LANES:(jj + 1) * LANES] = y.astype(bf16)
        else:
            jj = j - 2 * DN_QK // LANES
            dv_ref[0, :, jj * LANES:(jj + 1) * LANES] = y.astype(bf16)

    bg = bg_ref[0]
    beta = _sigmoid(bg)
    xb = bg + bias_ref[...]
    softplus = jnp.maximum(xb, 0.0) + jnp.log1p(jnp.exp(-jnp.abs(xb)))
    g = -jnp.exp(alog_ref[...]) * softplus
    r = lax.broadcasted_iota(jnp.int32, (tm, tm), 0)
    c = lax.broadcasted_iota(jnp.int32, (tm, tm), 1)
    same = (r // DN_CHUNK) == (c // DN_CHUNK)
    lower = jnp.where(same & (r >= c), 1.0, 0.0).astype(bf16)
    upper = jnp.where(same & (r <= c), 1.0, 0.0).astype(bf16)
    pieces = _split3(g)
    pref = sum(jnp.dot(lower, p, preferred_element_type=f32) for p in pieces)
    suff = sum(jnp.dot(upper, p, preferred_element_type=f32) for p in pieces)
    gc = jnp.where(lane < 2 * DN_HEADS + DN_HEADS, pref, suff)
    gates_ref[0] = jnp.where(lane < 2 * DN_HEADS, beta, gc)


def _prep(a, d, bg, qg, kg, cos_t, sin_t, cw, alog, bias, tm):
    b, s, _ = a.shape
    n_tiles = s // tm
    halo = BF16_SUBLANES
    per = tm // halo
    last = s // halo - 1
    kern = functools.partial(_prep_kernel, tm=tm, n_tiles=n_tiles)
    return pl.pallas_call(
        kern,
        grid=(b, n_tiles),
        in_specs=[
            pl.BlockSpec((1, tm, W_A), lambda bi, i: (bi, i, 0)),
            pl.BlockSpec((1, tm, W_D), lambda bi, i: (bi, i, 0)),
            pl.BlockSpec((1, halo, W_D), lambda bi, i: (bi, jnp.maximum(i * per - 1, 0), 0)),
            pl.BlockSpec((1, halo, W_D), lambda bi, i: (bi, jnp.minimum((i + 1) * per, last), 0)),
            pl.BlockSpec((1, tm, W_BG), lambda bi, i: (bi, i, 0)),
            _resident((1, LANES)),
            _resident((1, LANES)),
            pl.BlockSpec((tm, LANES), lambda bi, i: (i, 0)),
            pl.BlockSpec((tm, LANES), lambda bi, i: (i, 0)),
            _resident((F32_SUBLANES, W_D)),
            _resident((1, LANES)),
            _resident((1, LANES)),
        ],
        out_specs=[
            pl.BlockSpec((1, A_HEADS, tm, A_HEAD_DIM), lambda bi, i: (bi, 0, i, 0)),
            pl.BlockSpec((1, A_KV_HEADS, tm, A_HEAD_DIM), lambda bi, i: (bi, 0, i, 0)),
            pl.BlockSpec((1, A_KV_HEADS, 1, LANES, tm), lambda bi, i: (bi, 0, i, 0, 0)),
            pl.BlockSpec((1, tm, DN_QK), lambda bi, i: (bi, i, 0)),
            pl.BlockSpec((1, tm, DN_QK), lambda bi, i: (bi, i, 0)),
            pl.BlockSpec((1, tm, DN_V), lambda bi, i: (bi, i, 0)),
            pl.BlockSpec((1, tm, LANES), lambda bi, i: (bi, i, 0)),
        ],
        out_shape=[
            jax.ShapeDtypeStruct((b, A_HEADS, s, A_HEAD_DIM), bf16),
            jax.ShapeDtypeStruct((b, A_KV_HEADS, s, A_HEAD_DIM), bf16),
            jax.ShapeDtypeStruct((b, A_KV_HEADS, n_tiles, LANES, tm), bf16),
            jax.ShapeDtypeStruct((b, s, DN_QK), bf16),
            jax.ShapeDtypeStruct((b, s, DN_QK), bf16),
            jax.ShapeDtypeStruct((b, s, DN_V), bf16),
            jax.ShapeDtypeStruct((b, s, LANES), f32),
        ],
        scratch_shapes=[pltpu.VMEM((tm + 2 * halo, W_D), f32)],
        compiler_params=_params(("parallel", "parallel")),
        name="prep",
    )(a, d, d, d, bg, qg, kg, cos_t, sin_t, cw, alog, bias)


def _attn_kernel(q_ref, k_ref, vt_ref, o_ref, m_ref, acc_ref, s_ref, *, tq, tk, n_kv):
    nq = A_GROUP * tq
    q = q_ref[0].reshape(nq, A_HEAD_DIM)
    m_ref[...] = jnp.full(m_ref.shape, NEG_BIG, f32)
    acc_ref[...] = jnp.zeros(acc_ref.shape, f32)

    def scores(j):
        rows = pl.ds(pl.multiple_of(j * tk, tk), tk)
        return _dot_nt(k_ref[0, 0, rows, :], q)

    def update(slot, j):
        s = s_ref[slot]
        m_old = m_ref[...]
        m_new = jnp.maximum(m_old, jnp.max(s, axis=0, keepdims=True))
        p = jnp.exp2(s - m_new).astype(bf16)
        acc_ref[...] = acc_ref[...] * jnp.exp2(m_old - m_new) + jnp.dot(
            vt_ref[0, 0, j], p, preferred_element_type=f32)
        m_ref[...] = m_new

    s_ref[0] = scores(0)

    def body(i, carry):
        j = 2 * i
        s_ref[1] = scores(j + 1)
        update(0, j)
        s_ref[0] = scores(j + 2)
        update(1, j + 1)
        return carry

    lax.fori_loop(0, n_kv // 2 - 1, body, 0)
    s_ref[1] = scores(n_kv - 1)
    update(0, n_kv - 2)
    update(1, n_kv - 1)

    acc_t = acc_ref[...].T
    o = acc_t[:, :A_HEAD_DIM] * (1.0 / acc_t[:, A_HEAD_DIM:A_HEAD_DIM + 1])
    for g in range(A_GROUP):
        o_ref[0, :, g * A_HEAD_DIM:(g + 1) * A_HEAD_DIM] = o[g * tq:(g + 1) * tq].astype(o_ref.dtype)


def _attention(q, k, vt, tq):
    b, _, s, _ = q.shape
    n_kv, tk = vt.shape[2], vt.shape[4]
    assert n_kv % 2 == 0 and n_kv >= 2
    kern = functools.partial(_attn_kernel, tq=tq, tk=tk, n_kv=n_kv)
    nq = A_GROUP * tq
    return pl.pallas_call(
        kern,
        grid=(b, A_KV_HEADS, s // tq),
        in_specs=[
            pl.BlockSpec((1, A_GROUP, tq, A_HEAD_DIM), lambda bi, h, i: (bi, h, i, 0)),
            pl.BlockSpec((1, 1, s, A_HEAD_DIM), lambda bi, h, i: (bi, h, 0, 0)),
            pl.BlockSpec((1, 1, n_kv, LANES, tk), lambda bi, h, i: (bi, h, 0, 0, 0)),
        ],
        out_specs=pl.BlockSpec((1, tq, A_GROUP * A_HEAD_DIM), lambda bi, h, i: (bi, i, h)),
        out_shape=jax.ShapeDtypeStruct((b, s, A_Q), bf16),
        scratch_shapes=[pltpu.VMEM((1, nq), f32), pltpu.VMEM((LANES, nq), f32),
                        pltpu.VMEM((2, tk, nq), f32)],
        compiler_params=_params(("parallel", "parallel", "arbitrary")),
        name="attn",
    )(q, k, vt)


PACK = 2 * HEADS_PER_STEP
PACK_LANES = PACK * DN_CHUNK
DIR_LANES = HEADS_PER_STEP * DN_CHUNK


def _block_diag(y, mask):
    yb = y.astype(bf16)
    return jnp.where(mask, jnp.concatenate([yb] * PACK, axis=0), jnp.zeros((), bf16))


def _mm4(x, y_bd):
    return jnp.dot(x.astype(bf16), y_bd, preferred_element_type=f32)


def _dn_local_kernel(q_ref, k_ref, v_ref, gcol_ref, grow_ref,
                     u_ref, w_ref, qd_ref, kd_ref, qkd_ref, eg_ref, *, n_sub):
    c = DN_CHUNK
    hp = HEADS_PER_STEP
    r = lax.broadcasted_iota(jnp.int32, (c, PACK_LANES), 0)
    lane = lax.broadcasted_iota(jnp.int32, (c, PACK_LANES), 1)
    cl = lane % c
    delta = jnp.where(lane >= DIR_LANES, cl - r, r - cl)
    incl = delta >= 0
    strict = delta > 0
    bd16 = (r // 16) == (cl // 16)
    eye = jnp.where(r == cl, 1.0, 0.0)
    rb = lax.broadcasted_iota(jnp.int32, (PACK_LANES, PACK_LANES), 0) // c
    cb = lax.broadcasted_iota(jnp.int32, (PACK_LANES, PACK_LANES), 1) // c
    bd_mask = rb == cb
    head0 = lax.broadcasted_iota(jnp.int32, (c, LANES), 1) < DN_DK
    zero_b = jnp.zeros((c, LANES), bf16)

    def pack(col0, col1):
        return jnp.where(head0, col0, col1)

    for t in range(n_sub):
        rows = slice(t * c, (t + 1) * c)
        q2 = q_ref[0, rows, :]
        k2 = k_ref[0, rows, :]
        qf = q2.astype(f32)
        kf = k2.astype(f32)
        vf = v_ref[0, rows, :].astype(f32)
        gcol = gcol_ref[0, 0, rows, :]
        gcr = grow_ref[0, 0, t]
        beta = [pack(gcol[:, d * hp:d * hp + 1], gcol[:, d * hp + 1:d * hp + 2]) for d in range(2)]
        gc = [pack(gcol[:, 2 * hp + d * hp:2 * hp + d * hp + 1],
                   gcol[:, 2 * hp + d * hp + 1:2 * hp + d * hp + 2]) for d in range(2)]

        zk = jnp.concatenate([jnp.where(head0, k2, zero_b), jnp.where(head0, zero_b, k2)], axis=0)
        prod = _dot_nt(jnp.concatenate([q2, k2], axis=0), zk)
        qk4 = jnp.concatenate([prod[:c], prod[:c]], axis=1)
        kk4 = jnp.concatenate([prod[c:], prod[c:]], axis=1)
        decay = jnp.exp(jnp.where(incl, jnp.concatenate(gc, axis=1) - gcr, -jnp.inf))
        a = jnp.where(strict, jnp.concatenate(beta, axis=1) * kk4 * decay, 0.0)

        d = jnp.where(bd16, a, 0.0)
        o = a - d
        d2 = _mm4(d, _block_diag(d, bd_mask))
        d2_bd = _block_diag(d2, bd_mask)
        d4 = _mm4(d2, d2_bd)
        d4_bd = _block_diag(d4, bd_mask)
        d8 = _mm4(d4, d4_bd)
        t0 = eye - d
        t0 = t0 + _mm4(t0, d2_bd)
        t0 = t0 + _mm4(t0, d4_bd)
        t0 = t0 + _mm4(t0, _block_diag(d8, bd_mask))
        m = _mm4(t0, _block_diag(o, bd_mask))
        m2 = _mm4(m, _block_diag(m, bd_mask))
        t1 = t0 + _mm4(m2, _block_diag(t0, bd_mask))
        t_inv = t1 - _mm4(m, _block_diag(t1, bd_mask))

        blocks = []
        for dd in range(2):
            x_v = (vf * beta[dd]).astype(bf16)
            x_w = (kf * beta[dd] * jnp.exp(gc[dd])).astype(bf16)
            for h in range(hp):
                keep = head0 if h == 0 else jnp.logical_not(head0)
                xv_h = jnp.where(keep, x_v, zero_b)
                xw_h = jnp.where(keep, x_w, zero_b)
                row = [zero_b] * (2 * 2)
                row[2 * dd], row[2 * dd + 1] = xv_h, xw_h
                blocks.append(jnp.concatenate(row, axis=1))
        uw = jnp.dot(t_inv.astype(bf16), jnp.concatenate(blocks, axis=0), preferred_element_type=f32)

        qkd = qk4 * decay
        for dd in range(2):
            tot = c - 1 if dd == 0 else 0
            gt = gc[dd][tot:tot + 1, :]
            u_ref[0, 0, dd, rows, :] = uw[:, 2 * dd * LANES:(2 * dd + 1) * LANES]
            w_ref[0, 0, dd, rows, :] = uw[:, (2 * dd + 1) * LANES:(2 * dd + 2) * LANES].astype(bf16)
            qd_ref[0, 0, dd, rows, :] = (qf * jnp.exp(gc[dd])).astype(bf16)
            kd_ref[0, 0, dd, rows, :] = (kf * jnp.exp(gt - gc[dd])).astype(bf16)
            qkd_ref[0, 0, dd, rows, :] = qkd[:, dd * DIR_LANES:(dd + 1) * DIR_LANES].astype(bf16)
            eg_ref[0, 0, dd, t] = jnp.exp(gt)


def _dn_local(dq, dk, dv, gcol, grow, n_sub):
    b, s, _ = dq.shape
    n_chunks = s // DN_CHUNK
    n_pairs = DN_HEADS // HEADS_PER_STEP
    rows = n_sub * DN_CHUNK
    kern = functools.partial(_dn_local_kernel, n_sub=n_sub)
    spec = pl.BlockSpec((1, rows, LANES), lambda bi, p, g: (bi, g, p))
    ospec = pl.BlockSpec((1, 1, 2, rows, LANES), lambda bi, p, g: (bi, p, 0, g, 0))
    big = lambda dt: jax.ShapeDtypeStruct((b, n_pairs, 2, s, LANES), dt)
    return pl.pallas_call(
        kern,
        grid=(b, n_pairs, n_chunks // n_sub),
        in_specs=[spec, spec, spec,
                  pl.BlockSpec((1, 1, rows, 4 * HEADS_PER_STEP), lambda bi, p, g: (bi, p, g, 0)),
                  pl.BlockSpec((1, 1, n_sub, 1, PACK_LANES), lambda bi, p, g: (bi, p, g, 0, 0))],
        out_specs=[ospec, ospec, ospec, ospec, ospec,
                   pl.BlockSpec((1, 1, 2, n_sub, 1, LANES), lambda bi, p, g: (bi, p, 0, g, 0, 0))],
        out_shape=[big(f32), big(bf16), big(bf16), big(bf16), big(bf16),
                   jax.ShapeDtypeStruct((b, n_pairs, 2, n_chunks, 1, LANES), f32)],
        compiler_params=_params(("parallel", "parallel", "parallel")),
        name="dn_local",
    )(dq, dk, dv, gcol, grow)


def _dn_scan_kernel(*refs, n_sub, n_pairs):
    ins = refs[:12]
    of_ref, ob_ref, s_ref = refs[12:]
    views = (ins[0::2], ins[1::2])
    outs = (of_ref, ob_ref)
    c = DN_CHUNK
    head0 = lax.broadcasted_iota(jnp.int32, (c, LANES), 1) < DN_DK
    rs = lax.broadcasted_iota(jnp.int32, (LANES, LANES), 0) // DN_DK
    cs = lax.broadcasted_iota(jnp.int32, (LANES, LANES), 1) // DN_DV
    state_bd = rs == cs
    zero_b = jnp.zeros((c, LANES), bf16)

    @pl.when(pl.program_id(1) == 0)
    def _():
        s_ref[...] = jnp.zeros(s_ref.shape, f32)

    for t in range(n_sub):
        for dd in range(2):
            tt = t if dd == 0 else n_sub - 1 - t
            rows = slice(tt * c, (tt + 1) * c)
            u_ref, w_ref, qd_ref, kd_ref, qkd_ref, eg_ref = views[dd]
            for p in range(n_pairs):
                state = s_ref[dd, p]
                ws = jnp.dot(jnp.concatenate([w_ref[0, p, 0, rows, :], qd_ref[0, p, 0, rows, :]], axis=0),
                             state.astype(bf16), preferred_element_type=f32)
                v_new = (u_ref[0, p, 0, rows, :] - ws[:c]).astype(bf16)
                v_bd = jnp.concatenate([jnp.where(head0, v_new, zero_b),
                                        jnp.where(head0, zero_b, v_new)], axis=0)
                o2 = ws[c:] + jnp.dot(qkd_ref[0, p, 0, rows, :], v_bd, preferred_element_type=f32)
                upd = _dot_tn(kd_ref[0, p, 0, rows, :], v_new)
                s_ref[dd, p] = jnp.where(state_bd, state * eg_ref[0, p, 0, tt] + upd, 0.0)
                outs[dd][0, rows, p * LANES:(p + 1) * LANES] = o2


def _dn_scan(u, w, qd, kd, qkd, eg, n_sub):
    b, n_pairs, _, s, _ = u.shape
    n_groups = s // (n_sub * DN_CHUNK)
    rows = n_sub * DN_CHUNK
    kern = functools.partial(_dn_scan_kernel, n_sub=n_sub, n_pairs=n_pairs)
    fwd = pl.BlockSpec((1, n_pairs, 1, rows, LANES), lambda bi, g: (bi, 0, 0, g, 0))
    bwd = pl.BlockSpec((1, n_pairs, 1, rows, LANES), lambda bi, g: (bi, 0, 1, n_groups - 1 - g, 0))
    egf = pl.BlockSpec((1, n_pairs, 1, n_sub, 1, LANES), lambda bi, g: (bi, 0, 0, g, 0, 0))
    egb = pl.BlockSpec((1, n_pairs, 1, n_sub, 1, LANES), lambda bi, g: (bi, 0, 1, n_groups - 1 - g, 0, 0))
    out_sd = jax.ShapeDtypeStruct((b, s, DN_V), f32)
    return pl.pallas_call(
        kern,
        grid=(b, n_groups),
        in_specs=[fwd, bwd] * 5 + [egf, egb],
        out_specs=[pl.BlockSpec((1, rows, DN_V), lambda bi, g: (bi, g, 0)),
                   pl.BlockSpec((1, rows, DN_V), lambda bi, g: (bi, n_groups - 1 - g, 0))],
        out_shape=[out_sd, out_sd],
        scratch_shapes=[pltpu.VMEM((2, n_pairs, LANES, LANES), f32)],
        compiler_params=_params(("parallel", "arbitrary")),
        name="dn_scan",
    )(u, u, w, w, qd, qd, kd, kd, qkd, qkd, eg, eg)


def _merge_kernel(oa_ref, of_ref, ob_ref, z_ref, gl_ref, x_ref, dng_ref, woa_ref, wod_ref, wout_ref, out_ref):
    ya = jnp.dot(oa_ref[...], woa_ref[...], preferred_element_type=f32)
    parts = []
    for j in range(DN_V // LANES):
        cols = slice(j * LANES, (j + 1) * LANES)
        o = of_ref[:, cols] + ob_ref[:, cols]
        ms = _group_sum(o * o, DN_DV) * (1.0 / DN_DV)
        z = z_ref[:, cols].astype(f32)
        parts.append((o * lax.rsqrt(ms + EPS) * dng_ref[...] * (z * _sigmoid(z))).astype(bf16))
    yd = jnp.dot(jnp.concatenate(parts, axis=1), wod_ref[...], preferred_element_type=f32)
    g0 = _sigmoid(gl_ref[:, :D_MODEL].astype(f32))
    g1 = _sigmoid(gl_ref[:, D_MODEL:].astype(f32))
    mixed = (g0 * ya + g1 * yd).astype(bf16)
    out_ref[...] = x_ref[...] + jnp.dot(mixed, wout_ref[...], preferred_element_type=f32)


def _merge(oa, of, ob, z, gl, x2, dng, woa, wod, wout, tm):
    t = x2.shape[0]
    row = lambda w: pl.BlockSpec((tm, w), lambda i: (i, 0))
    return pl.pallas_call(
        _merge_kernel,
        grid=(t // tm,),
        in_specs=[row(A_Q), row(DN_V), row(DN_V), row(W_Z), row(W_G), row(D_MODEL),
                  _resident((1, LANES)), _resident((A_Q, D_MODEL)), _resident((DN_V, D_MODEL)),
                  _resident((D_MODEL, D_MODEL))],
        out_specs=row(D_MODEL),
        out_shape=jax.ShapeDtypeStruct((t, D_MODEL), f32),
        compiler_params=_params(("parallel",)),
        name="merge",
    )(oa, of, ob, z, gl, x2, dng, woa, wod, wout)


def _ffn_kernel(x_ref, xprev_ref, xnext_ref, g_ref, wg_ref, wv_ref, cg_ref, cv_ref, wd_ref, out_ref,
                h_ref, ug_ref, uv_ref, *, tm, n_tiles):
    i = pl.program_id(1)
    halo = F32_SUBLANES

    def norm(x):
        ms = jnp.mean(x * x, axis=-1, keepdims=True)
        return x * lax.rsqrt(ms + EPS) * g_ref[...]

    h_ref[0:halo, :] = jnp.where(i > 0, norm(xprev_ref[0]), 0.0).astype(bf16)
    h_ref[halo:halo + tm, :] = norm(x_ref[0]).astype(bf16)
    h_ref[halo + tm:, :] = jnp.where(i < n_tiles - 1, norm(xnext_ref[0]), 0.0).astype(bf16)
    h = h_ref[...]

    def conv(u_ref, cw):
        acc = None
        for kk in range(FFN_CONV_W):
            term = u_ref[pl.ds(halo - FFN_CONV_W // 2 + kk, tm), :] * cw[kk:kk + 1, :]
            acc = term if acc is None else acc + term
        return acc

    acc = x_ref[0]
    for j in range(D_FF // FF_CHUNK):
        ug_ref[...] = jnp.dot(h, wg_ref[j], preferred_element_type=f32)
        uv_ref[...] = jnp.dot(h, wv_ref[j], preferred_element_type=f32)
        gate = conv(ug_ref, cg_ref[j])
        val = conv(uv_ref, cv_ref[j])
        act = (gate * _sigmoid(gate) * val).astype(bf16)
        acc = acc + jnp.dot(act, wd_ref[j], preferred_element_type=f32)
    out_ref[0] = acc


def _ffn(x3, g, wg, wv, cg, cv, wd, tm):
    b, s, _ = x3.shape
    n_tiles = s // tm
    halo = F32_SUBLANES
    per = tm // halo
    last = s // halo - 1
    n_ff = D_FF // FF_CHUNK
    kern = functools.partial(_ffn_kernel, tm=tm, n_tiles=n_tiles)
    return pl.pallas_call(
        kern,
        grid=(b, n_tiles),
        in_specs=[
            pl.BlockSpec((1, tm, D_MODEL), lambda bi, i: (bi, i, 0)),
            pl.BlockSpec((1, halo, D_MODEL), lambda bi, i: (bi, jnp.maximum(i * per - 1, 0), 0)),
            pl.BlockSpec((1, halo, D_MODEL), lambda bi, i: (bi, jnp.minimum((i + 1) * per, last), 0)),
            _resident((1, D_MODEL)),
            _resident((n_ff, D_MODEL, FF_CHUNK)),
            _resident((n_ff, D_MODEL, FF_CHUNK)),
            _resident((n_ff, F32_SUBLANES, FF_CHUNK)),
            _resident((n_ff, F32_SUBLANES, FF_CHUNK)),
            _resident((n_ff, FF_CHUNK, D_MODEL)),
        ],
        out_specs=pl.BlockSpec((1, tm, D_MODEL), lambda bi, i: (bi, i, 0)),
        out_shape=jax.ShapeDtypeStruct((b, s, D_MODEL), f32),
        scratch_shapes=[pltpu.VMEM((tm + 2 * halo, D_MODEL), bf16),
                        pltpu.VMEM((tm + 2 * halo, FF_CHUNK), f32),
                        pltpu.VMEM((tm + 2 * halo, FF_CHUNK), f32)],
        compiler_params=_params(("parallel", "parallel")),
        name="ffn",
    )(x3, x3, x3, g, wg, wv, cg, cv, wd)


def _rope_tables(s):
    rows_n = s // GRID_W
    row = jnp.repeat(jnp.arange(rows_n), GRID_W).astype(f32)
    col = jnp.tile(jnp.arange(GRID_W), rows_n).astype(f32)
    inv_freq = ROPE_THETA ** (-jnp.arange(ROPE_FREQS, dtype=f32) / ROPE_FREQS)
    ang_r = row[:, None] * inv_freq
    ang_c = col[:, None] * inv_freq
    cos64 = jnp.concatenate([jnp.cos(ang_r), jnp.cos(ang_r), jnp.cos(ang_c), jnp.cos(ang_c)], axis=1)
    sin64 = jnp.concatenate([-jnp.sin(ang_r), jnp.sin(ang_r), -jnp.sin(ang_c), jnp.sin(ang_c)], axis=1)
    return jnp.tile(cos64, (1, 2)), jnp.tile(sin64, (1, 2))


def _pad_rows(w, rows):
    return jnp.pad(w, ((0, rows - w.shape[0]), (0, 0)))


def _lane_row(vals, offset):
    return jnp.pad(vals.astype(f32), (offset, LANES - offset - vals.shape[0]))[None, :]


def kernel(x, norm_mix_g, w_in, q_norm_g, k_norm_g, dn_conv_w, dn_a_log, dn_dt_bias, dn_out_norm_g,
           w_o_attn, w_o_dn, w_out, norm_ffn_g, w_up, ffn_conv_w, w_down):
    b, s, d = x.shape
    assert d == D_MODEL and s % 512 == 0 and s % GRID_W == 0
    t = b * s
    tm = 512
    n_chunks = s // DN_CHUNK
    n_pairs = DN_HEADS // HEADS_PER_STEP
    hp = HEADS_PER_STEP
    cos_t, sin_t = _rope_tables(s)
    offs = np.cumsum((A_Q, A_KV, A_KV, DN_QK, DN_QK, DN_V, 2 * DN_HEADS, 2 * DN_HEADS, DN_V))
    n_ff = D_FF // FF_CHUNK

    for l in range(DEPTH):
        wl = w_in[l]
        w_cat = jnp.concatenate(
            [wl[:, :offs[5]], wl[:, offs[7]:offs[8]], wl[:, offs[8]:], wl[:, offs[5]:offs[7]],
             jnp.zeros((D_MODEL, LANES - 4 * DN_HEADS), wl.dtype)], axis=1).astype(bf16)
        a, dd, z, gl, bg = _inproj(x.reshape(t, d), norm_mix_g[l][None, :], w_cat, tm)

        q, k, vt, dq, dk, dv, gates = _prep(
            a.reshape(b, s, W_A), dd.reshape(b, s, W_D), bg.reshape(b, s, W_BG),
            jnp.tile(q_norm_g[l], 2)[None, :], jnp.tile(k_norm_g[l], 2)[None, :], cos_t, sin_t,
            _pad_rows(dn_conv_w[l], F32_SUBLANES),
            _lane_row(dn_a_log[l].reshape(-1), 2 * DN_HEADS),
            _lane_row(dn_dt_bias[l].reshape(-1), 2 * DN_HEADS), tm)

        o_attn = _attention(q, k, vt, 256)

        g4 = gates[:, :, :4 * DN_HEADS].reshape(b, s, 4, n_pairs, hp)
        gcol = jnp.transpose(g4, (0, 3, 1, 2, 4)).reshape(b, n_pairs, s, 4 * hp)
        grow = jnp.transpose(g4[:, :, 2:].reshape(b, n_chunks, DN_CHUNK, 2, n_pairs, hp),
                             (0, 4, 1, 3, 5, 2)).reshape(b, n_pairs, n_chunks, 1, PACK_LANES)
        o_f, o_b = _dn_scan(*_dn_local(dq, dk, dv, gcol, grow, 4), 8)

        x = _merge(o_attn.reshape(t, A_Q), o_f.reshape(t, DN_V), o_b.reshape(t, DN_V), z, gl,
                   x.reshape(t, d),
                   jnp.tile(dn_out_norm_g[l], 2)[None, :], w_o_attn[l].astype(bf16),
                   w_o_dn[l].astype(bf16), w_out[l].astype(bf16), tm).reshape(b, s, d)

        wu = w_up[l].astype(bf16)
        wg = jnp.transpose(wu[:, :D_FF].reshape(d, n_ff, FF_CHUNK), (1, 0, 2))
        wv = jnp.transpose(wu[:, D_FF:].reshape(d, n_ff, FF_CHUNK), (1, 0, 2))
        cwp = _pad_rows(ffn_conv_w[l], F32_SUBLANES)
        cg = jnp.transpose(cwp[:, :D_FF].reshape(F32_SUBLANES, n_ff, FF_CHUNK), (1, 0, 2))
        cv = jnp.transpose(cwp[:, D_FF:].reshape(F32_SUBLANES, n_ff, FF_CHUNK), (1, 0, 2))
        wd = w_down[l].astype(bf16).reshape(n_ff, FF_CHUNK, d)
        x = _ffn(x, norm_ffn_g[l][None, :], wg, wv, cg, cv, wd, tm)
    return x
```

```python
import functools
import math

import jax
import jax.numpy as jnp
import numpy as np
from jax import lax
from jax.experimental import pallas as pl
from jax.experimental.pallas import tpu as pltpu

D_MODEL = 1024
DEPTH = 4
GRID_W = 64
EPS = 1e-6

A_HEADS = 8
A_KV_HEADS = 2
A_HEAD_DIM = 64
A_GROUP = A_HEADS // A_KV_HEADS
ROPE_THETA = 10000.0
ROPE_FREQS = A_HEAD_DIM // 4

DN_HEADS = 8
DN_DK = 64
DN_DV = 64
DN_CONV_W = 5
DN_CHUNK = 64

D_FF = 2816
FFN_CONV_W = 3

A_Q = A_HEADS * A_HEAD_DIM
A_KV = A_KV_HEADS * A_HEAD_DIM
DN_QK = DN_HEADS * DN_DK
DN_V = DN_HEADS * DN_DV

LANES = 128
BF16_SUBLANES = 16
F32_SUBLANES = 8
VMEM_LIMIT = 56 * 1024 * 1024

W_A = A_Q + 2 * A_KV
W_D = 2 * DN_QK + DN_V
W_Z = DN_V
W_G = 2 * D_MODEL
W_BG = LANES
HEADS_PER_STEP = 2
FF_CHUNK = 256

LOG2E = 1.4426950408889634
NEG_BIG = -1e30

bf16 = jnp.bfloat16
f32 = jnp.float32


def _params(sem):
    return pltpu.CompilerParams(dimension_semantics=sem, vmem_limit_bytes=VMEM_LIMIT)


def _resident(shape):
    nd = len(shape)
    return pl.BlockSpec(shape, lambda *_: (0,) * nd, pipeline_mode=pl.Buffered(1))


def _dot(a, b):
    return jnp.dot(a.astype(bf16), b.astype(bf16), preferred_element_type=f32)


def _dot_nt(a, b):
    return lax.dot_general(a.astype(bf16), b.astype(bf16), (((1,), (1,)), ((), ())),
                           preferred_element_type=f32)


def _dot_tn(a, b):
    return lax.dot_general(a.astype(bf16), b.astype(bf16), (((0,), (0,)), ((), ())),
                           preferred_element_type=f32)


def _sigmoid(x):
    return 1.0 / (1.0 + jnp.exp(-x))


def _split3(x):
    hi = x.astype(bf16)
    r1 = x - hi.astype(f32)
    mid = r1.astype(bf16)
    lo = (r1 - mid.astype(f32)).astype(bf16)
    return hi, mid, lo


def _group_sum(x, width):
    r = lax.broadcasted_iota(jnp.int32, (LANES, LANES), 0) // width
    c = lax.broadcasted_iota(jnp.int32, (LANES, LANES), 1) // width
    ones_bd = jnp.where(r == c, 1.0, 0.0).astype(bf16)
    hi, mid, lo = _split3(x)
    return (jnp.dot(hi, ones_bd, preferred_element_type=f32)
            + jnp.dot(mid, ones_bd, preferred_element_type=f32)
            + jnp.dot(lo, ones_bd, preferred_element_type=f32))


def _inproj_kernel(x_ref, g_ref, w_ref, oa_ref, od_ref, oz_ref, og_ref, obg_ref):
    x = x_ref[...]
    ms = jnp.mean(x * x, axis=-1, keepdims=True)
    h = (x * lax.rsqrt(ms + EPS) * g_ref[...]).astype(bf16)
    c = 0
    for o_ref in (oa_ref, od_ref, oz_ref, og_ref, obg_ref):
        n = o_ref.shape[-1]
        o_ref[...] = jnp.dot(h, w_ref[:, c:c + n], preferred_element_type=f32).astype(o_ref.dtype)
        c += n


def _inproj(x2, g, w_cat, tm):
    t = x2.shape[0]
    n_cat = w_cat.shape[1]
    widths = (W_A, W_D, W_Z, W_G, W_BG)
    dtypes = (bf16, bf16, bf16, bf16, f32)
    return pl.pallas_call(
        _inproj_kernel,
        grid=(t // tm,),
        in_specs=[pl.BlockSpec((tm, D_MODEL), lambda i: (i, 0)),
                  _resident((1, D_MODEL)),
                  _resident((D_MODEL, n_cat))],
        out_specs=[pl.BlockSpec((tm, w), lambda i: (i, 0)) for w in widths],
        out_shape=[jax.ShapeDtypeStruct((t, w), d) for w, d in zip(widths, dtypes)],
        compiler_params=_params(("parallel",)),
        name="inproj",
    )(x2, g, w_cat)


def _swap16(x):
    lane = lax.broadcasted_iota(jnp.int32, x.shape, 1)
    return jnp.where((lane % 32) < 16, pltpu.roll(x, LANES - 16, 1), pltpu.roll(x, 16, 1))


def _prep_kernel(a_ref, d_ref, dprev_ref, dnext_ref, bg_ref, qg_ref, kg_ref, cos_ref, sin_ref,
                 cw_ref, alog_ref, bias_ref,
                 q_ref, k_ref, vt_ref, dq_ref, dk_ref, dv_ref, gates_ref,
                 xs_ref, *, tm, n_tiles):
    i = pl.program_id(1)
    cos = cos_ref[...]
    sin = sin_ref[...]
    lane = lax.broadcasted_iota(jnp.int32, (tm, LANES), 1)

    def norm_rope(x, g):
        ms = _group_sum(x * x, A_HEAD_DIM) * (1.0 / A_HEAD_DIM)
        y = x * lax.rsqrt(ms + EPS) * g
        return y * cos + _swap16(y) * sin

    qscale = (A_HEAD_DIM ** -0.5) * LOG2E
    for j in range(A_Q // LANES):
        y = norm_rope(a_ref[0, :, j * LANES:(j + 1) * LANES].astype(f32), qg_ref[...]) * qscale
        q_ref[0, 2 * j] = y[:, :A_HEAD_DIM].astype(bf16)
        q_ref[0, 2 * j + 1] = y[:, A_HEAD_DIM:].astype(bf16)
    y = norm_rope(a_ref[0, :, A_Q:A_Q + LANES].astype(f32), kg_ref[...])
    k_ref[0, 0] = y[:, :A_HEAD_DIM].astype(bf16)
    k_ref[0, 1] = y[:, A_HEAD_DIM:].astype(bf16)
    v = a_ref[0, :, A_Q + A_KV:A_Q + 2 * A_KV].astype(f32)
    ones_col = jnp.where(lane == A_HEAD_DIM, 1.0, 0.0)
    vt_ref[0, 0, 0] = jnp.where(lane < A_HEAD_DIM, v, ones_col).T.astype(bf16)
    vt_ref[0, 1, 0] = jnp.where(lane < A_HEAD_DIM, pltpu.roll(v, A_HEAD_DIM, 1), ones_col).T.astype(bf16)

    halo = BF16_SUBLANES
    xs_ref[0:halo, :] = jnp.where(i > 0, dprev_ref[0].astype(f32), 0.0)
    xs_ref[halo:halo + tm, :] = d_ref[0].astype(f32)
    xs_ref[halo + tm:, :] = jnp.where(i < n_tiles - 1, dnext_ref[0].astype(f32), 0.0)
    pad = DN_CONV_W // 2
    for j in range(W_D // LANES):
        cols = slice(j * LANES, (j + 1) * LANES)
        acc = None
        for kk in range(DN_CONV_W):
            term = xs_ref[pl.ds(halo - pad + kk, tm), cols] * cw_ref[kk:kk + 1, cols]
            acc = term if acc is None else acc + term
        y = acc * _sigmoid(acc)
        if j < 2 * DN_QK // LANES:
            y = y * lax.rsqrt(_group_sum(y * y, DN_DK) + EPS)
        if j < DN_QK // LANES:
            dq_ref[0, :, cols] = (y * (DN_DK ** -0.5)).astype(bf16)
        elif j < 2 * DN_QK // LANES:
            jj = j - DN_QK // LANES
            dk_ref[0, :, jj * LANES:(jj + 1) * LANES] = y.astype(bf16)
        else:
            jj = j - 2 * DN_QK // LANES
            dv_ref[0, :, jj * LANES:(jj + 1) * LANES] = y.astype(bf16)

    bg = bg_ref[0]
    beta = _sigmoid(bg)
    xb = bg + bias_ref[...]
    softplus = jnp.maximum(xb, 0.0) + jnp.log1p(jnp.exp(-jnp.abs(xb)))
    g = -jnp.exp(alog_ref[...]) * softplus
    r = lax.broadcasted_iota(jnp.int32, (tm, tm), 0)
    c = lax.broadcasted_iota(jnp.int32, (tm, tm), 1)
    same = (r // DN_CHUNK) == (c // DN_CHUNK)
    lower = jnp.where(same & (r >= c), 1.0, 0.0).astype(bf16)
    upper = jnp.where(same & (r <= c), 1.0, 0.0).astype(bf16)
    pieces = _split3(g)
    pref = sum(jnp.dot(lower, p, preferred_element_type=f32) for p in pieces)
    suff = sum(jnp.dot(upper, p, preferred_element_type=f32) for p in pieces)
    gc = jnp.where(lane < 2 * DN_HEADS + DN_HEADS, pref, suff)
    gates_ref[0] = jnp.where(lane < 2 * DN_HEADS, beta, gc)


def _prep(a, d, bg, qg, kg, cos_t, sin_t, cw, alog, bias, tm):
    b, s, _ = a.shape
    n_tiles = s // tm
    halo = BF16_SUBLANES
    per = tm // halo
    last = s // halo - 1
    kern = functools.partial(_prep_kernel, tm=tm, n_tiles=n_tiles)
    return pl.pallas_call(
        kern,
        grid=(b, n_tiles),
        in_specs=[
            pl.BlockSpec((1, tm, W_A), lambda bi, i: (bi, i, 0)),
            pl.BlockSpec((1, tm, W_D), lambda bi, i: (bi, i, 0)),
            pl.BlockSpec((1, halo, W_D), lambda bi, i: (bi, jnp.maximum(i * per - 1, 0), 0)),
            pl.BlockSpec((1, halo, W_D), lambda bi, i: (bi, jnp.minimum((i + 1) * per, last), 0)),
            pl.BlockSpec((1, tm, W_BG), lambda bi, i: (bi, i, 0)),
            _resident((1, LANES)),
            _resident((1, LANES)),
            pl.BlockSpec((tm, LANES), lambda bi, i: (i, 0)),
            pl.BlockSpec((tm, LANES), lambda bi, i: (i, 0)),
            _resident((F32_SUBLANES, W_D)),
            _resident((1, LANES)),
            _resident((1, LANES)),
        ],
        out_specs=[
            pl.BlockSpec((1, A_HEADS, tm, A_HEAD_DIM), lambda bi, i: (bi, 0, i, 0)),
            pl.BlockSpec((1, A_KV_HEADS, tm, A_HEAD_DIM), lambda bi, i: (bi, 0, i, 0)),
            pl.BlockSpec((1, A_KV_HEADS, 1, LANES, tm), lambda bi, i: (bi, 0, i, 0, 0)),
            pl.BlockSpec((1, tm, DN_QK), lambda bi, i: (bi, i, 0)),
            pl.BlockSpec((1, tm, DN_QK), lambda bi, i: (bi, i, 0)),
            pl.BlockSpec((1, tm, DN_V), lambda bi, i: (bi, i, 0)),
            pl.BlockSpec((1, tm, LANES), lambda bi, i: (bi, i, 0)),
        ],
        out_shape=[
            jax.ShapeDtypeStruct((b, A_HEADS, s, A_HEAD_DIM), bf16),
            jax.ShapeDtypeStruct((b, A_KV_HEADS, s, A_HEAD_DIM), bf16),
            jax.ShapeDtypeStruct((b, A_KV_HEADS, n_tiles, LANES, tm), bf16),
            jax.ShapeDtypeStruct((b, s, DN_QK), bf16),
            jax.ShapeDtypeStruct((b, s, DN_QK), bf16),
            jax.ShapeDtypeStruct((b, s, DN_V), bf16),
            jax.ShapeDtypeStruct((b, s, LANES), f32),
        ],
        scratch_shapes=[pltpu.VMEM((tm + 2 * halo, W_D), f32)],
        compiler_params=_params(("parallel", "parallel")),
        name="prep",
    )(a, d, d, d, bg, qg, kg, cos_t, sin_t, cw, alog, bias)


def _attn_kernel(q_ref, k_ref, vt_ref, o_ref, m_ref, acc_ref, s_ref, *, tq, tk, n_kv):
    nq = A_GROUP * tq
    q = q_ref[0].reshape(nq, A_HEAD_DIM)
    m_ref[...] = jnp.full(m_ref.shape, NEG_BIG, f32)
    acc_ref[...] = jnp.zeros(acc_ref.shape, f32)

    def scores(j):
        rows = pl.ds(pl.multiple_of(j * tk, tk), tk)
        return _dot_nt(k_ref[0, 0, rows, :], q)

    def update(slot, j):
        s = s_ref[slot]
        m_old = m_ref[...]
        m_new = jnp.maximum(m_old, jnp.max(s, axis=0, keepdims=True))
        p = jnp.exp2(s - m_new).astype(bf16)
        acc_ref[...] = acc_ref[...] * jnp.exp2(m_old - m_new) + jnp.dot(
            vt_ref[0, 0, j], p, preferred_element_type=f32)
        m_ref[...] = m_new

    s_ref[0] = scores(0)

    def body(i, carry):
        j = 2 * i
        s_ref[1] = scores(j + 1)
        update(0, j)
        s_ref[0] = scores(j + 2)
        update(1, j + 1)
        return carry

    lax.fori_loop(0, n_kv // 2 - 1, body, 0)
    s_ref[1] = scores(n_kv - 1)
    update(0, n_kv - 2)
    update(1, n_kv - 1)

    acc_t = acc_ref[...].T
    o = acc_t[:, :A_HEAD_DIM] * (1.0 / acc_t[:, A_HEAD_DIM:A_HEAD_DIM + 1])
    for g in range(A_GROUP):
        o_ref[0, :, g * A_HEAD_DIM:(g + 1) * A_HEAD_DIM] = o[g * tq:(g + 1) * tq].astype(o_ref.dtype)


def _attention(q, k, vt, tq):
    b, _, s, _ = q.shape
    n_kv, tk = vt.shape[2], vt.shape[4]
    assert n_kv % 2 == 0 and n_kv >= 2
    kern = functools.partial(_attn_kernel, tq=tq, tk=tk, n_kv=n_kv)
    nq = A_GROUP * tq
    return pl.pallas_call(
        kern,
        grid=(b, A_KV_HEADS, s // tq),
        in_specs=[
            pl.BlockSpec((1, A_GROUP, tq, A_HEAD_DIM), lambda bi, h, i: (bi, h, i, 0)),
            pl.BlockSpec((1, 1, s, A_HEAD_DIM), lambda bi, h, i: (bi, h, 0, 0)),
            pl.BlockSpec((1, 1, n_kv, LANES, tk), lambda bi, h, i: (bi, h, 0, 0, 0)),
        ],
        out_specs=pl.BlockSpec((1, tq, A_GROUP * A_HEAD_DIM), lambda bi, h, i: (bi, i, h)),
        out_shape=jax.ShapeDtypeStruct((b, s, A_Q), bf16),
        scratch_shapes=[pltpu.VMEM((1, nq), f32), pltpu.VMEM((LANES, nq), f32),
                        pltpu.VMEM((2, tk, nq), f32)],
        compiler_params=_params(("parallel", "parallel", "arbitrary")),
        name="attn",
    )(q, k, vt)


PACK = 2 * HEADS_PER_STEP
PACK_LANES = PACK * DN_CHUNK
DIR_LANES = HEADS_PER_STEP * DN_CHUNK


def _block_diag(y, mask):
    yb = y.astype(bf16)
    return jnp.where(mask, jnp.concatenate([yb] * PACK, axis=0), jnp.zeros((), bf16))


def _mm4(x, y_bd):
    return jnp.dot(x.astype(bf16), y_bd, preferred_element_type=f32)


def _dn_local_kernel(q_ref, k_ref, v_ref, gcol_ref, grow_ref,
                     u_ref, w_ref, qd_ref, kd_ref, qkd_ref, eg_ref, *, n_sub):
    c = DN_CHUNK
    hp = HEADS_PER_STEP
    r = lax.broadcasted_iota(jnp.int32, (c, PACK_LANES), 0)
    lane = lax.broadcasted_iota(jnp.int32, (c, PACK_LANES), 1)
    cl = lane % c
    delta = jnp.where(lane >= DIR_LANES, cl - r, r - cl)
    incl = delta >= 0
    strict = delta > 0
    bd16 = (r // 16) == (cl // 16)
    eye = jnp.where(r == cl, 1.0, 0.0)
    rb = lax.broadcasted_iota(jnp.int32, (PACK_LANES, PACK_LANES), 0) // c
    cb = lax.broadcasted_iota(jnp.int32, (PACK_LANES, PACK_LANES), 1) // c
    bd_mask = rb == cb
    head0 = lax.broadcasted_iota(jnp.int32, (c, LANES), 1) < DN_DK
    zero_b = jnp.zeros((c, LANES), bf16)

    def pack(col0, col1):
        return jnp.where(head0, col0, col1)

    def gates(t):
        gcol = gcol_ref[0, 0, t * c:(t + 1) * c, :]
        beta = [pack(gcol[:, d * hp:d * hp + 1], gcol[:, d * hp + 1:d * hp + 2]) for d in range(2)]
        gc = [pack(gcol[:, 2 * hp + d * hp:2 * hp + d * hp + 1],
                   gcol[:, 2 * hp + d * hp + 1:2 * hp + d * hp + 2]) for d in range(2)]
        return beta, gc

    chunks = range(n_sub)
    a = []
    for t in chunks:
        rows = slice(t * c, (t + 1) * c)
        q2 = q_ref[0, rows, :]
        k2 = k_ref[0, rows, :]
        qf = q2.astype(f32)
        kf = k2.astype(f32)
        gcr = grow_ref[0, 0, t]
        beta, gc = gates(t)
        zk = jnp.concatenate([jnp.where(head0, k2, zero_b), jnp.where(head0, zero_b, k2)], axis=0)
        prod = _dot_nt(jnp.concatenate([q2, k2], axis=0), zk)
        qk4 = jnp.concatenate([prod[:c], prod[:c]], axis=1)
        kk4 = jnp.concatenate([prod[c:], prod[c:]], axis=1)
        decay = jnp.exp(jnp.where(incl, jnp.concatenate(gc, axis=1) - gcr, -jnp.inf))
        a.append(jnp.where(strict, jnp.concatenate(beta, axis=1) * kk4 * decay, 0.0))
        qkd = qk4 * decay
        for dd in range(2):
            tot = c - 1 if dd == 0 else 0
            gt = gc[dd][tot:tot + 1, :]
            qd_ref[0, 0, dd, rows, :] = (qf * jnp.exp(gc[dd])).astype(bf16)
            kd_ref[0, 0, dd, rows, :] = (kf * jnp.exp(gt - gc[dd])).astype(bf16)
            qkd_ref[0, 0, dd, rows, :] = qkd[:, dd * DIR_LANES:(dd + 1) * DIR_LANES].astype(bf16)
            eg_ref[0, 0, dd, t] = jnp.exp(gt)

    bd = lambda xs: [_block_diag(x, bd_mask) for x in xs]
    mm = lambda xs, ys: [_mm4(x, y) for x, y in zip(xs, ys)]
    add = lambda xs, ys: [x + y for x, y in zip(xs, ys)]
    d = [jnp.where(bd16, x, 0.0) for x in a]
    o = [x - y for x, y in zip(a, d)]
    d2 = mm(d, bd(d))
    d2_bd = bd(d2)
    d4 = mm(d2, d2_bd)
    d4_bd = bd(d4)
    d8 = mm(d4, d4_bd)
    t0 = [eye - x for x in d]
    t0 = add(t0, mm(t0, d2_bd))
    t0 = add(t0, mm(t0, d4_bd))
    t0 = add(t0, mm(t0, bd(d8)))
    m = mm(t0, bd(o))
    m2 = mm(m, bd(m))
    t1 = add(t0, mm(m2, bd(t0)))
    t_inv = [x - y for x, y in zip(t1, mm(m, bd(t1)))]

    for t in chunks:
        rows = slice(t * c, (t + 1) * c)
        kf = k_ref[0, rows, :].astype(f32)
        vf = v_ref[0, rows, :].astype(f32)
        beta, gc = gates(t)
        blocks = []
        for dd in range(2):
            x_v = (vf * beta[dd]).astype(bf16)
            x_w = (kf * beta[dd] * jnp.exp(gc[dd])).astype(bf16)
            for h in range(hp):
                keep = head0 if h == 0 else jnp.logical_not(head0)
                row = [zero_b] * (2 * 2)
                row[2 * dd], row[2 * dd + 1] = jnp.where(keep, x_v, zero_b), jnp.where(keep, x_w, zero_b)
                blocks.append(jnp.concatenate(row, axis=1))
        uw = jnp.dot(t_inv[t].astype(bf16), jnp.concatenate(blocks, axis=0), preferred_element_type=f32)
        for dd in range(2):
            u_ref[0, 0, dd, rows, :] = uw[:, 2 * dd * LANES:(2 * dd + 1) * LANES]
            w_ref[0, 0, dd, rows, :] = uw[:, (2 * dd + 1) * LANES:(2 * dd + 2) * LANES].astype(bf16)


def _dn_local(dq, dk, dv, gcol, grow, n_sub):
    b, s, _ = dq.shape
    n_chunks = s // DN_CHUNK
    n_pairs = DN_HEADS // HEADS_PER_STEP
    rows = n_sub * DN_CHUNK
    kern = functools.partial(_dn_local_kernel, n_sub=n_sub)
    spec = pl.BlockSpec((1, rows, LANES), lambda bi, p, g: (bi, g, p))
    ospec = pl.BlockSpec((1, 1, 2, rows, LANES), lambda bi, p, g: (bi, p, 0, g, 0))
    big = lambda dt: jax.ShapeDtypeStruct((b, n_pairs, 2, s, LANES), dt)
    return pl.pallas_call(
        kern,
        grid=(b, n_pairs, n_chunks // n_sub),
        in_specs=[spec, spec, spec,
                  pl.BlockSpec((1, 1, rows, 4 * HEADS_PER_STEP), lambda bi, p, g: (bi, p, g, 0)),
                  pl.BlockSpec((1, 1, n_sub, 1, PACK_LANES), lambda bi, p, g: (bi, p, g, 0, 0))],
        out_specs=[ospec, ospec, ospec, ospec, ospec,
                   pl.BlockSpec((1, 1, 2, n_sub, 1, LANES), lambda bi, p, g: (bi, p, 0, g, 0, 0))],
        out_shape=[big(f32), big(bf16), big(bf16), big(bf16), big(bf16),
                   jax.ShapeDtypeStruct((b, n_pairs, 2, n_chunks, 1, LANES), f32)],
        compiler_params=_params(("parallel", "parallel", "parallel")),
        name="dn_local",
    )(dq, dk, dv, gcol, grow)


def _dn_scan_kernel(*refs, n_sub, n_pairs):
    ins = refs[:12]
    of_ref, ob_ref, s_ref = refs[12:]
    views = (ins[0::2], ins[1::2])
    outs = (of_ref, ob_ref)
    c = DN_CHUNK
    head0 = lax.broadcasted_iota(jnp.int32, (c, LANES), 1) < DN_DK
    rs = lax.broadcasted_iota(jnp.int32, (LANES, LANES), 0) // DN_DK
    cs = lax.broadcasted_iota(jnp.int32, (LANES, LANES), 1) // DN_DV
    state_bd = rs == cs
    zero_b = jnp.zeros((c, LANES), bf16)

    @pl.when(pl.program_id(1) == 0)
    def _():
        s_ref[...] = jnp.zeros(s_ref.shape, f32)

    chains = [(dd, p) for dd in range(2) for p in range(n_pairs)]
    for t in range(n_sub):
        def rows(dd):
            tt = t if dd == 0 else n_sub - 1 - t
            return tt, slice(tt * c, (tt + 1) * c)

        def rd(dd, p, which):
            return views[dd][which][0, p, 0, rows(dd)[1], :]

        state = [s_ref[dd, p] for dd, p in chains]
        ws = [jnp.dot(jnp.concatenate([rd(dd, p, 1), rd(dd, p, 2)], axis=0), st.astype(bf16),
                      preferred_element_type=f32) for (dd, p), st in zip(chains, state)]
        v_new = [(rd(dd, p, 0) - x[:c]).astype(bf16) for (dd, p), x in zip(chains, ws)]
        upd = [_dot_tn(rd(dd, p, 3), v) for (dd, p), v in zip(chains, v_new)]
        for (dd, p), st, x in zip(chains, state, upd):
            s_ref[dd, p] = jnp.where(state_bd, st * views[dd][5][0, p, 0, rows(dd)[0]] + x, 0.0)
        for (dd, p), x, v in zip(chains, ws, v_new):
            v_bd = jnp.concatenate([jnp.where(head0, v, zero_b), jnp.where(head0, zero_b, v)], axis=0)
            outs[dd][0, rows(dd)[1], p * LANES:(p + 1) * LANES] = x[c:] + jnp.dot(
                rd(dd, p, 4), v_bd, preferred_element_type=f32)


def _dn_scan(u, w, qd, kd, qkd, eg, n_sub):
    b, n_pairs, _, s, _ = u.shape
    n_groups = s // (n_sub * DN_CHUNK)
    rows = n_sub * DN_CHUNK
    kern = functools.partial(_dn_scan_kernel, n_sub=n_sub, n_pairs=n_pairs)
    fwd = pl.BlockSpec((1, n_pairs, 1, rows, LANES), lambda bi, g: (bi, 0, 0, g, 0))
    bwd = pl.BlockSpec((1, n_pairs, 1, rows, LANES), lambda bi, g: (bi, 0, 1, n_groups - 1 - g, 0))
    egf = pl.BlockSpec((1, n_pairs, 1, n_sub, 1, LANES), lambda bi, g: (bi, 0, 0, g, 0, 0))
    egb = pl.BlockSpec((1, n_pairs, 1, n_sub, 1, LANES), lambda bi, g: (bi, 0, 1, n_groups - 1 - g, 0, 0))
    out_sd = jax.ShapeDtypeStruct((b, s, DN_V), f32)
    return pl.pallas_call(
        kern,
        grid=(b, n_groups),
        in_specs=[fwd, bwd] * 5 + [egf, egb],
        out_specs=[pl.BlockSpec((1, rows, DN_V), lambda bi, g: (bi, g, 0)),
                   pl.BlockSpec((1, rows, DN_V), lambda bi, g: (bi, n_groups - 1 - g, 0))],
        out_shape=[out_sd, out_sd],
        scratch_shapes=[pltpu.VMEM((2, n_pairs, LANES, LANES), f32)],
        compiler_params=_params(("parallel", "arbitrary")),
        name="dn_scan",
    )(u, u, w, w, qd, qd, kd, kd, qkd, qkd, eg, eg)


def _merge_kernel(oa_ref, of_ref, ob_ref, z_ref, gl_ref, x_ref, dng_ref, woa_ref, wod_ref, wout_ref, out_ref):
    ya = jnp.dot(oa_ref[...], woa_ref[...], preferred_element_type=f32)
    parts = []
    for j in range(DN_V // LANES):
        cols = slice(j * LANES, (j + 1) * LANES)
        o = of_ref[:, cols] + ob_ref[:, cols]
        ms = _group_sum(o * o, DN_DV) * (1.0 / DN_DV)
        z = z_ref[:, cols].astype(f32)
        parts.append((o * lax.rsqrt(ms + EPS) * dng_ref[...] * (z * _sigmoid(z))).astype(bf16))
    yd = jnp.dot(jnp.concatenate(parts, axis=1), wod_ref[...], preferred_element_type=f32)
    g0 = _sigmoid(gl_ref[:, :D_MODEL].astype(f32))
    g1 = _sigmoid(gl_ref[:, D_MODEL:].astype(f32))
    mixed = (g0 * ya + g1 * yd).astype(bf16)
    out_ref[...] = x_ref[...] + jnp.dot(mixed, wout_ref[...], preferred_element_type=f32)


def _merge(oa, of, ob, z, gl, x2, dng, woa, wod, wout, tm):
    t = x2.shape[0]
    row = lambda w: pl.BlockSpec((tm, w), lambda i: (i, 0))
    return pl.pallas_call(
        _merge_kernel,
        grid=(t // tm,),
        in_specs=[row(A_Q), row(DN_V), row(DN_V), row(W_Z), row(W_G), row(D_MODEL),
                  _resident((1, LANES)), _resident((A_Q, D_MODEL)), _resident((DN_V, D_MODEL)),
                  _resident((D_MODEL, D_MODEL))],
        out_specs=row(D_MODEL),
        out_shape=jax.ShapeDtypeStruct((t, D_MODEL), f32),
        compiler_params=_params(("parallel",)),
        name="merge",
    )(oa, of, ob, z, gl, x2, dng, woa, wod, wout)


def _ffn_kernel(x_ref, xprev_ref, xnext_ref, g_ref, wg_ref, wv_ref, cg_ref, cv_ref, wd_ref, out_ref,
                h_ref, ug_ref, uv_ref, *, tm, n_tiles):
    i = pl.program_id(1)
    halo = F32_SUBLANES

    def norm(x):
        ms = jnp.mean(x * x, axis=-1, keepdims=True)
        return x * lax.rsqrt(ms + EPS) * g_ref[...]

    h_ref[0:halo, :] = jnp.where(i > 0, norm(xprev_ref[0]), 0.0).astype(bf16)
    h_ref[halo:halo + tm, :] = norm(x_ref[0]).astype(bf16)
    h_ref[halo + tm:, :] = jnp.where(i < n_tiles - 1, norm(xnext_ref[0]), 0.0).astype(bf16)
    h = h_ref[...]

    def conv(u_ref, cw):
        acc = None
        for kk in range(FFN_CONV_W):
            term = u_ref[pl.ds(halo - FFN_CONV_W // 2 + kk, tm), :] * cw[kk:kk + 1, :]
            acc = term if acc is None else acc + term
        return acc

    acc = x_ref[0]
    for j in range(D_FF // FF_CHUNK):
        ug_ref[...] = jnp.dot(h, wg_ref[j], preferred_element_type=f32)
        uv_ref[...] = jnp.dot(h, wv_ref[j], preferred_element_type=f32)
        gate = conv(ug_ref, cg_ref[j])
        val = conv(uv_ref, cv_ref[j])
        act = (gate * _sigmoid(gate) * val).astype(bf16)
        acc = acc + jnp.dot(act, wd_ref[j], preferred_element_type=f32)
    out_ref[0] = acc


def _ffn(x3, g, wg, wv, cg, cv, wd, tm):
    b, s, _ = x3.shape
    n_tiles = s // tm
    halo = F32_SUBLANES
    per = tm // halo
    last = s // halo - 1
    n_ff = D_FF // FF_CHUNK
    kern = functools.partial(_ffn_kernel, tm=tm, n_tiles=n_tiles)
    return pl.pallas_call(
        kern,
        grid=(b, n_tiles),
        in_specs=[
            pl.BlockSpec((1, tm, D_MODEL), lambda bi, i: (bi, i, 0)),
            pl.BlockSpec((1, halo, D_MODEL), lambda bi, i: (bi, jnp.maximum(i * per - 1, 0), 0)),
            pl.BlockSpec((1, halo, D_MODEL), lambda bi, i: (bi, jnp.minimum((i + 1) * per, last), 0)),
            _resident((1, D_MODEL)),
            _resident((n_ff, D_MODEL, FF_CHUNK)),
            _resident((n_ff, D_MODEL, FF_CHUNK)),
            _resident((n_ff, F32_SUBLANES, FF_CHUNK)),
            _resident((n_ff, F32_SUBLANES, FF_CHUNK)),
            _resident((n_ff, FF_CHUNK, D_MODEL)),
        ],
        out_specs=pl.BlockSpec((1, tm, D_MODEL), lambda bi, i: (bi, i, 0)),
        out_shape=jax.ShapeDtypeStruct((b, s, D_MODEL), f32),
        scratch_shapes=[pltpu.VMEM((tm + 2 * halo, D_MODEL), bf16),
                        pltpu.VMEM((tm + 2 * halo, FF_CHUNK), f32),
                        pltpu.VMEM((tm + 2 * halo, FF_CHUNK), f32)],
        compiler_params=_params(("parallel", "parallel")),
        name="ffn",
    )(x3, x3, x3, g, wg, wv, cg, cv, wd)


def _rope_tables(s):
    rows_n = s // GRID_W
    row = jnp.repeat(jnp.arange(rows_n), GRID_W).astype(f32)
    col = jnp.tile(jnp.arange(GRID_W), rows_n).astype(f32)
    inv_freq = ROPE_THETA ** (-jnp.arange(ROPE_FREQS, dtype=f32) / ROPE_FREQS)
    ang_r = row[:, None] * inv_freq
    ang_c = col[:, None] * inv_freq
    cos64 = jnp.concatenate([jnp.cos(ang_r), jnp.cos(ang_r), jnp.cos(ang_c), jnp.cos(ang_c)], axis=1)
    sin64 = jnp.concatenate([-jnp.sin(ang_r), jnp.sin(ang_r), -jnp.sin(ang_c), jnp.sin(ang_c)], axis=1)
    return jnp.tile(cos64, (1, 2)), jnp.tile(sin64, (1, 2))


def _pad_rows(w, rows):
    return jnp.pad(w, ((0, rows - w.shape[0]), (0, 0)))


def _lane_row(vals, offset):
    return jnp.pad(vals.astype(f32), (offset, LANES - offset - vals.shape[0]))[None, :]


def kernel(x, norm_mix_g, w_in, q_norm_g, k_norm_g, dn_conv_w, dn_a_log, dn_dt_bias, dn_out_norm_g,
           w_o_attn, w_o_dn, w_out, norm_ffn_g, w_up, ffn_conv_w, w_down):
    b, s, d = x.shape
    assert d == D_MODEL and s % 512 == 0 and s % GRID_W == 0
    t = b * s
    tm = 512
    n_chunks = s // DN_CHUNK
    n_pairs = DN_HEADS // HEADS_PER_STEP
    hp = HEADS_PER_STEP
    cos_t, sin_t = _rope_tables(s)
    offs = np.cumsum((A_Q, A_KV, A_KV, DN_QK, DN_QK, DN_V, 2 * DN_HEADS, 2 * DN_HEADS, DN_V))
    n_ff = D_FF // FF_CHUNK

    for l in range(DEPTH):
        wl = w_in[l]
        w_cat = jnp.concatenate(
            [wl[:, :offs[5]], wl[:, offs[7]:offs[8]], wl[:, offs[8]:], wl[:, offs[5]:offs[7]],
             jnp.zeros((D_MODEL, LANES - 4 * DN_HEADS), wl.dtype)], axis=1).astype(bf16)
        a, dd, z, gl, bg = _inproj(x.reshape(t, d), norm_mix_g[l][None, :], w_cat, tm)

        q, k, vt, dq, dk, dv, gates = _prep(
            a.reshape(b, s, W_A), dd.reshape(b, s, W_D), bg.reshape(b, s, W_BG),
            jnp.tile(q_norm_g[l], 2)[None, :], jnp.tile(k_norm_g[l], 2)[None, :], cos_t, sin_t,
            _pad_rows(dn_conv_w[l], F32_SUBLANES),
            _lane_row(dn_a_log[l].reshape(-1), 2 * DN_HEADS),
            _lane_row(dn_dt_bias[l].reshape(-1), 2 * DN_HEADS), tm)

        o_attn = _attention(q, k, vt, 256)

        g4 = gates[:, :, :4 * DN_HEADS].reshape(b, s, 4, n_pairs, hp)
        gcol = jnp.transpose(g4, (0, 3, 1, 2, 4)).reshape(b, n_pairs, s, 4 * hp)
        grow = jnp.transpose(g4[:, :, 2:].reshape(b, n_chunks, DN_CHUNK, 2, n_pairs, hp),
                             (0, 4, 1, 3, 5, 2)).reshape(b, n_pairs, n_chunks, 1, PACK_LANES)
        o_f, o_b = _dn_scan(*_dn_local(dq, dk, dv, gcol, grow, 8), 8)

        x = _merge(o_attn.reshape(t, A_Q), o_f.reshape(t, DN_V), o_b.reshape(t, DN_V), z, gl,
                   x.reshape(t, d),
                   jnp.tile(dn_out_norm_g[l], 2)[None, :], w_o_attn[l].astype(bf16),
                   w_o_dn[l].astype(bf16), w_out[l].astype(bf16), tm).reshape(b, s, d)

        wu = w_up[l].astype(bf16)
        wg = jnp.transpose(wu[:, :D_FF].reshape(d, n_ff, FF_CHUNK), (1, 0, 2))
        wv = jnp.transpose(wu[:, D_FF:].reshape(d, n_ff, FF_CHUNK), (1, 0, 2))
        cwp = _pad_rows(ffn_conv_w[l], F32_SUBLANES)
        cg = jnp.transpose(cwp[:, :D_FF].reshape(F32_SUBLANES, n_ff, FF_CHUNK), (1, 0, 2))
        cv = jnp.transpose(cwp[:, D_FF:].reshape(F32_SUBLANES, n_ff, FF_CHUNK), (1, 0, 2))
        wd = w_down[l].astype(bf16).reshape(n_ff, FF_CHUNK, d)
        x = _ffn(x, norm_ffn_g[l][None, :], wg, wv, cg, cv, wd, tm)
    return x
```

```python
import functools
import math

import jax
import jax.numpy as jnp
import numpy as np
from jax import lax
from jax.experimental import pallas as pl
from jax.experimental.pallas import tpu as pltpu

D_MODEL = 1024
DEPTH = 4
GRID_W = 64
EPS = 1e-6

A_HEADS = 8
A_KV_HEADS = 2
A_HEAD_DIM = 64
A_GROUP = A_HEADS // A_KV_HEADS
ROPE_THETA = 10000.0
ROPE_FREQS = A_HEAD_DIM // 4

DN_HEADS = 8
DN_DK = 64
DN_DV = 64
DN_CONV_W = 5
DN_CHUNK = 64

D_FF = 2816
FFN_CONV_W = 3

A_Q = A_HEADS * A_HEAD_DIM
A_KV = A_KV_HEADS * A_HEAD_DIM
DN_QK = DN_HEADS * DN_DK
DN_V = DN_HEADS * DN_DV

LANES = 128
BF16_SUBLANES = 16
F32_SUBLANES = 8
VMEM_LIMIT = 56 * 1024 * 1024

W_A = A_Q + 2 * A_KV
W_D = 2 * DN_QK + DN_V
W_Z = DN_V
W_G = 2 * D_MODEL
W_BG = LANES
VT_ROWS = A_HEAD_DIM + BF16_SUBLANES
HEADS_PER_STEP = 2
FF_CHUNK = 256

LOG2E = 1.4426950408889634
NEG_BIG = -1e30

bf16 = jnp.bfloat16
f32 = jnp.float32


def _params(sem):
    return pltpu.CompilerParams(dimension_semantics=sem, vmem_limit_bytes=VMEM_LIMIT)


def _resident(shape):
    nd = len(shape)
    return pl.BlockSpec(shape, lambda *_: (0,) * nd, pipeline_mode=pl.Buffered(1))


def _dot(a, b):
    return jnp.dot(a.astype(bf16), b.astype(bf16), preferred_element_type=f32)


def _dot_nt(a, b):
    return lax.dot_general(a.astype(bf16), b.astype(bf16), (((1,), (1,)), ((), ())),
                           preferred_element_type=f32)


def _dot_tn(a, b):
    return lax.dot_general(a.astype(bf16), b.astype(bf16), (((0,), (0,)), ((), ())),
                           preferred_element_type=f32)


def _sigmoid(x):
    return 1.0 / (1.0 + jnp.exp(-x))


def _split3(x):
    hi = x.astype(bf16)
    r1 = x - hi.astype(f32)
    mid = r1.astype(bf16)
    lo = (r1 - mid.astype(f32)).astype(bf16)
    return hi, mid, lo


def _group_sum(x, width):
    r = lax.broadcasted_iota(jnp.int32, (LANES, LANES), 0) // width
    c = lax.broadcasted_iota(jnp.int32, (LANES, LANES), 1) // width
    ones_bd = jnp.where(r == c, 1.0, 0.0).astype(bf16)
    hi, mid, lo = _split3(x)
    return (jnp.dot(hi, ones_bd, preferred_element_type=f32)
            + jnp.dot(mid, ones_bd, preferred_element_type=f32)
            + jnp.dot(lo, ones_bd, preferred_element_type=f32))


def _inproj_kernel(x_ref, g_ref, w_ref, oa_ref, od_ref, oz_ref, og_ref, obg_ref):
    x = x_ref[...]
    ms = jnp.mean(x * x, axis=-1, keepdims=True)
    h = (x * lax.rsqrt(ms + EPS) * g_ref[...]).astype(bf16)
    c = 0
    for o_ref in (oa_ref, od_ref, oz_ref, og_ref, obg_ref):
        n = o_ref.shape[-1]
        o_ref[...] = jnp.dot(h, w_ref[:, c:c + n], preferred_element_type=f32).astype(o_ref.dtype)
        c += n


def _inproj(x2, g, w_cat, tm):
    t = x2.shape[0]
    n_cat = w_cat.shape[1]
    widths = (W_A, W_D, W_Z, W_G, W_BG)
    dtypes = (bf16, bf16, bf16, bf16, f32)
    return pl.pallas_call(
        _inproj_kernel,
        grid=(t // tm,),
        in_specs=[pl.BlockSpec((tm, D_MODEL), lambda i: (i, 0)),
                  _resident((1, D_MODEL)),
                  _resident((D_MODEL, n_cat))],
        out_specs=[pl.BlockSpec((tm, w), lambda i: (i, 0)) for w in widths],
        out_shape=[jax.ShapeDtypeStruct((t, w), d) for w, d in zip(widths, dtypes)],
        compiler_params=_params(("parallel",)),
        name="inproj",
    )(x2, g, w_cat)


def _swap16(x):
    lane = lax.broadcasted_iota(jnp.int32, x.shape, 1)
    return jnp.where((lane % 32) < 16, pltpu.roll(x, LANES - 16, 1), pltpu.roll(x, 16, 1))


def _prep_kernel(a_ref, d_ref, dprev_ref, dnext_ref, bg_ref, qg_ref, kg_ref, cos_ref, sin_ref,
                 cw_ref, alog_ref, bias_ref,
                 q_ref, k_ref, vt_ref, dq_ref, dk_ref, dv_ref, gates_ref,
                 xs_ref, *, tm, n_tiles):
    i = pl.program_id(1)
    cos = cos_ref[...]
    sin = sin_ref[...]
    lane = lax.broadcasted_iota(jnp.int32, (tm, LANES), 1)

    def norm_rope(x, g):
        ms = _group_sum(x * x, A_HEAD_DIM) * (1.0 / A_HEAD_DIM)
        y = x * lax.rsqrt(ms + EPS) * g
        return y * cos + _swap16(y) * sin

    qscale = (A_HEAD_DIM ** -0.5) * LOG2E
    for j in range(A_Q // LANES):
        y = norm_rope(a_ref[0, :, j * LANES:(j + 1) * LANES].astype(f32), qg_ref[...]) * qscale
        q_ref[0, 2 * j] = y[:, :A_HEAD_DIM].astype(bf16)
        q_ref[0, 2 * j + 1] = y[:, A_HEAD_DIM:].astype(bf16)
    y = norm_rope(a_ref[0, :, A_Q:A_Q + LANES].astype(f32), kg_ref[...])
    k_ref[0, 0] = y[:, :A_HEAD_DIM].astype(bf16)
    k_ref[0, 1] = y[:, A_HEAD_DIM:].astype(bf16)
    v = a_ref[0, :, A_Q + A_KV:A_Q + 2 * A_KV].astype(f32)
    ones_col = jnp.where(lane == A_HEAD_DIM, 1.0, 0.0)
    vt_ref[0, 0, 0] = jnp.where(lane < A_HEAD_DIM, v, ones_col).T[:VT_ROWS].astype(bf16)
    vt_ref[0, 1, 0] = jnp.where(lane < A_HEAD_DIM, pltpu.roll(v, A_HEAD_DIM, 1),
                                ones_col).T[:VT_ROWS].astype(bf16)

    halo = BF16_SUBLANES
    xs_ref[0:halo, :] = jnp.where(i > 0, dprev_ref[0].astype(f32), 0.0)
    xs_ref[halo:halo + tm, :] = d_ref[0].astype(f32)
    xs_ref[halo + tm:, :] = jnp.where(i < n_tiles - 1, dnext_ref[0].astype(f32), 0.0)
    pad = DN_CONV_W // 2
    for j in range(W_D // LANES):
        cols = slice(j * LANES, (j + 1) * LANES)
        acc = None
        for kk in range(DN_CONV_W):
            term = xs_ref[pl.ds(halo - pad + kk, tm), cols] * cw_ref[kk:kk + 1, cols]
            acc = term if acc is None else acc + term
        y = acc * _sigmoid(acc)
        if j < 2 * DN_QK // LANES:
            y = y * lax.rsqrt(_group_sum(y * y, DN_DK) + EPS)
        if j < DN_QK // LANES:
            dq_ref[0, :, cols] = (y * (DN_DK ** -0.5)).astype(bf16)
        elif j < 2 * DN_QK // LANES:
            jj = j - DN_QK // LANES
            dk_ref[0, :, jj * LANES:(jj + 1) * LANES] = y.astype(bf16)
        else:
            jj = j - 2 * DN_QK // LANES
            dv_ref[0, :, jj * LANES:(jj + 1) * LANES] = y.astype(bf16)

    bg = bg_ref[0]
    beta = _sigmoid(bg)
    xb = bg + bias_ref[...]
    softplus = jnp.maximum(xb, 0.0) + jnp.log1p(jnp.exp(-jnp.abs(xb)))
    g = -jnp.exp(alog_ref[...]) * softplus
    r = lax.broadcasted_iota(jnp.int32, (tm, tm), 0)
    c = lax.broadcasted_iota(jnp.int32, (tm, tm), 1)
    same = (r // DN_CHUNK) == (c // DN_CHUNK)
    lower = jnp.where(same & (r >= c), 1.0, 0.0).astype(bf16)
    upper = jnp.where(same & (r <= c), 1.0, 0.0).astype(bf16)
    pieces = _split3(g)
    pref = sum(jnp.dot(lower, p, preferred_element_type=f32) for p in pieces)
    suff = sum(jnp.dot(upper, p, preferred_element_type=f32) for p in pieces)
    gc = jnp.where(lane < 2 * DN_HEADS + DN_HEADS, pref, suff)
    gates_ref[0] = jnp.where(lane < 2 * DN_HEADS, beta, gc)


def _prep(a, d, bg, qg, kg, cos_t, sin_t, cw, alog, bias, tm):
    b, s, _ = a.shape
    n_tiles = s // tm
    halo = BF16_SUBLANES
    per = tm // halo
    last = s // halo - 1
    kern = functools.partial(_prep_kernel, tm=tm, n_tiles=n_tiles)
    return pl.pallas_call(
        kern,
        grid=(b, n_tiles),
        in_specs=[
            pl.BlockSpec((1, tm, W_A), lambda bi, i: (bi, i, 0)),
            pl.BlockSpec((1, tm, W_D), lambda bi, i: (bi, i, 0)),
            pl.BlockSpec((1, halo, W_D), lambda bi, i: (bi, jnp.maximum(i * per - 1, 0), 0)),
            pl.BlockSpec((1, halo, W_D), lambda bi, i: (bi, jnp.minimum((i + 1) * per, last), 0)),
            pl.BlockSpec((1, tm, W_BG), lambda bi, i: (bi, i, 0)),
            _resident((1, LANES)),
            _resident((1, LANES)),
            pl.BlockSpec((tm, LANES), lambda bi, i: (i, 0)),
            pl.BlockSpec((tm, LANES), lambda bi, i: (i, 0)),
            _resident((F32_SUBLANES, W_D)),
            _resident((1, LANES)),
            _resident((1, LANES)),
        ],
        out_specs=[
            pl.BlockSpec((1, A_HEADS, tm, A_HEAD_DIM), lambda bi, i: (bi, 0, i, 0)),
            pl.BlockSpec((1, A_KV_HEADS, tm, A_HEAD_DIM), lambda bi, i: (bi, 0, i, 0)),
            pl.BlockSpec((1, A_KV_HEADS, 1, VT_ROWS, tm), lambda bi, i: (bi, 0, i, 0, 0)),
            pl.BlockSpec((1, tm, DN_QK), lambda bi, i: (bi, i, 0)),
            pl.BlockSpec((1, tm, DN_QK), lambda bi, i: (bi, i, 0)),
            pl.BlockSpec((1, tm, DN_V), lambda bi, i: (bi, i, 0)),
            pl.BlockSpec((1, tm, LANES), lambda bi, i: (bi, i, 0)),
        ],
        out_shape=[
            jax.ShapeDtypeStruct((b, A_HEADS, s, A_HEAD_DIM), bf16),
            jax.ShapeDtypeStruct((b, A_KV_HEADS, s, A_HEAD_DIM), bf16),
            jax.ShapeDtypeStruct((b, A_KV_HEADS, n_tiles, VT_ROWS, tm), bf16),
            jax.ShapeDtypeStruct((b, s, DN_QK), bf16),
            jax.ShapeDtypeStruct((b, s, DN_QK), bf16),
            jax.ShapeDtypeStruct((b, s, DN_V), bf16),
            jax.ShapeDtypeStruct((b, s, LANES), f32),
        ],
        scratch_shapes=[pltpu.VMEM((tm + 2 * halo, W_D), f32)],
        compiler_params=_params(("parallel", "parallel")),
        name="prep",
    )(a, d, d, d, bg, qg, kg, cos_t, sin_t, cw, alog, bias)


def _attn_kernel(q_ref, k_ref, vt_ref, o_ref, m_ref, acc_ref, s_ref, *, tq, tk, n_kv):
    nq = A_GROUP * tq
    q = q_ref[0].reshape(nq, A_HEAD_DIM)
    m_ref[...] = jnp.full(m_ref.shape, NEG_BIG, f32)
    acc_ref[...] = jnp.zeros(acc_ref.shape, f32)

    def scores(j, slot):
        rows = pl.ds(pl.multiple_of(j * tk, tk), tk)
        s_ref[slot] = _dot_nt(k_ref[0, 0, rows, :], q)

    def update(slot, j):
        s = s_ref[slot]
        m_old = m_ref[...]
        m_new = jnp.maximum(m_old, jnp.max(s, axis=0, keepdims=True))
        p = jnp.exp2(s - m_new).astype(bf16)
        acc_ref[...] = acc_ref[...] * jnp.exp2(m_old - m_new) + jnp.dot(
            vt_ref[0, 0, j], p, preferred_element_type=f32)
        m_ref[...] = m_new

    scores(0, 0)

    def body(i, carry):
        j = 2 * i
        scores(j + 1, 1)
        update(0, j)
        scores(j + 2, 0)
        update(1, j + 1)
        return carry

    lax.fori_loop(0, n_kv // 2 - 1, body, 0)
    scores(n_kv - 1, 1)
    update(0, n_kv - 2)
    update(1, n_kv - 1)

    acc_t = jnp.concatenate([acc_ref[...], jnp.zeros((LANES - VT_ROWS, nq), f32)], axis=0).T
    o = acc_t[:, :A_HEAD_DIM] * (1.0 / acc_t[:, A_HEAD_DIM:A_HEAD_DIM + 1])
    for g in range(A_GROUP):
        o_ref[0, :, g * A_HEAD_DIM:(g + 1) * A_HEAD_DIM] = o[g * tq:(g + 1) * tq].astype(o_ref.dtype)


def _attention(q, k, vt, tq):
    b, _, s, _ = q.shape
    n_kv, tk = vt.shape[2], vt.shape[4]
    assert n_kv % 2 == 0 and n_kv >= 2
    kern = functools.partial(_attn_kernel, tq=tq, tk=tk, n_kv=n_kv)
    nq = A_GROUP * tq
    return pl.pallas_call(
        kern,
        grid=(b, A_KV_HEADS, s // tq),
        in_specs=[
            pl.BlockSpec((1, A_GROUP, tq, A_HEAD_DIM), lambda bi, h, i: (bi, h, i, 0)),
            pl.BlockSpec((1, 1, s, A_HEAD_DIM), lambda bi, h, i: (bi, h, 0, 0)),
            pl.BlockSpec((1, 1, n_kv, VT_ROWS, tk), lambda bi, h, i: (bi, h, 0, 0, 0)),
        ],
        out_specs=pl.BlockSpec((1, tq, A_GROUP * A_HEAD_DIM), lambda bi, h, i: (bi, i, h)),
        out_shape=jax.ShapeDtypeStruct((b, s, A_Q), bf16),
        scratch_shapes=[pltpu.VMEM((1, nq), f32), pltpu.VMEM((VT_ROWS, nq), f32),
                        pltpu.VMEM((2, tk, nq), f32)],
        compiler_params=_params(("parallel", "parallel", "arbitrary")),
        name="attn",
    )(q, k, vt)


PACK = 2 * HEADS_PER_STEP
PACK_LANES = PACK * DN_CHUNK
DIR_LANES = HEADS_PER_STEP * DN_CHUNK


def _block_diag(y, mask):
    yb = y.astype(bf16)
    return jnp.where(mask, jnp.concatenate([yb] * PACK, axis=0), jnp.zeros((), bf16))


def _mm4(x, y_bd):
    return jnp.dot(x.astype(bf16), y_bd, preferred_element_type=f32)


def _dn_local_kernel(q_ref, k_ref, v_ref, gcol_ref, grow_ref,
                     u_ref, w_ref, qd_ref, kd_ref, qkd_ref, eg_ref, *, n_sub):
    c = DN_CHUNK
    hp = HEADS_PER_STEP
    r = lax.broadcasted_iota(jnp.int32, (c, PACK_LANES), 0)
    lane = lax.broadcasted_iota(jnp.int32, (c, PACK_LANES), 1)
    cl = lane % c
    delta = jnp.where(lane >= DIR_LANES, cl - r, r - cl)
    incl = delta >= 0
    strict = delta > 0
    bd16 = (r // 16) == (cl // 16)
    eye = jnp.where(r == cl, 1.0, 0.0)
    rb = lax.broadcasted_iota(jnp.int32, (PACK_LANES, PACK_LANES), 0) // c
    cb = lax.broadcasted_iota(jnp.int32, (PACK_LANES, PACK_LANES), 1) // c
    bd_mask = rb == cb
    head0 = lax.broadcasted_iota(jnp.int32, (c, LANES), 1) < DN_DK
    zero_b = jnp.zeros((c, LANES), bf16)

    def pack(col0, col1):
        return jnp.where(head0, col0, col1)

    def gates(t):
        gcol = gcol_ref[0, 0, t * c:(t + 1) * c, :]
        beta = [pack(gcol[:, d * hp:d * hp + 1], gcol[:, d * hp + 1:d * hp + 2]) for d in range(2)]
        gc = [pack(gcol[:, 2 * hp + d * hp:2 * hp + d * hp + 1],
                   gcol[:, 2 * hp + d * hp + 1:2 * hp + d * hp + 2]) for d in range(2)]
        return beta, gc

    chunks = range(n_sub)
    a = []
    gate_vals = []
    for t in chunks:
        rows = slice(t * c, (t + 1) * c)
        q2 = q_ref[0, rows, :]
        k2 = k_ref[0, rows, :]
        qf = q2.astype(f32)
        kf = k2.astype(f32)
        gcr = grow_ref[0, 0, t]
        beta, gc = gates(t)
        gate_vals.append((beta, gc))
        zk =jnp.concatenate([jnp.where(head0, k2, zero_b), jnp.where(head0, zero_b, k2)], axis=0)
        prod = _dot_nt(jnp.concatenate([q2, k2], axis=0), zk)
        qk4 = jnp.concatenate([prod[:c], prod[:c]], axis=1)
        kk4 = jnp.concatenate([prod[c:], prod[c:]], axis=1)
        decay = jnp.exp(jnp.where(incl, jnp.concatenate(gc, axis=1) - gcr, -jnp.inf))
        a.append(jnp.where(strict, jnp.concatenate(beta, axis=1) * kk4 * decay, 0.0))
        qkd = qk4 * decay
        for dd in range(2):
            tot = c - 1 if dd == 0 else 0
            gt = gc[dd][tot:tot + 1, :]
            qd_ref[0, 0, dd, rows, :] = (qf * jnp.exp(gc[dd])).astype(bf16)
            kd_ref[0, 0, dd, rows, :] = (kf * jnp.exp(gt - gc[dd])).astype(bf16)
            qkd_ref[0, 0, dd, rows, :] = qkd[:, dd * DIR_LANES:(dd + 1) * DIR_LANES].astype(bf16)
            eg_ref[0, 0, dd, t] = jnp.exp(gt)

    bd = lambda xs: [_block_diag(x, bd_mask) for x in xs]
    mm = lambda xs, ys: [_mm4(x, y) for x, y in zip(xs, ys)]
    add = lambda xs, ys: [x + y for x, y in zip(xs, ys)]
    d = [jnp.where(bd16, x, 0.0) for x in a]
    o = [x - y for x, y in zip(a, d)]
    d2 = mm(d, bd(d))
    d2_bd = bd(d2)
    d4 = mm(d2, d2_bd)
    d4_bd = bd(d4)
    d8 = mm(d4, d4_bd)
    t0 = [eye - x for x in d]
    t0 = add(t0, mm(t0, d2_bd))
    t0 = add(t0, mm(t0, d4_bd))
    t0 = add(t0, mm(t0, bd(d8)))
    m = mm(t0, bd(o))
    m2 = mm(m, bd(m))
    t1 = add(t0, mm(m2, bd(t0)))
    t_inv = [x - y for x, y in zip(t1, mm(m, bd(t1)))]

    for t in chunks:
        rows = slice(t * c, (t + 1) * c)
        kf = k_ref[0, rows, :].astype(f32)
        vf = v_ref[0, rows, :].astype(f32)
        beta, gc = gate_vals[t]
        blocks = []
        for dd in range(2):
            x_v = (vf * beta[dd]).astype(bf16)
            x_w = (kf * beta[dd] * jnp.exp(gc[dd])).astype(bf16)
            for h in range(hp):
                keep = head0 if h == 0 else jnp.logical_not(head0)
                row = [zero_b] * (2 * 2)
                row[2 * dd], row[2 * dd + 1] = jnp.where(keep, x_v, zero_b), jnp.where(keep, x_w, zero_b)
                blocks.append(jnp.concatenate(row, axis=1))
        uw = jnp.dot(t_inv[t].astype(bf16), jnp.concatenate(blocks, axis=0), preferred_element_type=f32)
        for dd in range(2):
            u_ref[0, 0, dd, rows, :] = uw[:, 2 * dd * LANES:(2 * dd + 1) * LANES]
            w_ref[0, 0, dd, rows, :] = uw[:, (2 * dd + 1) * LANES:(2 * dd + 2) * LANES].astype(bf16)


def _dn_local(dq, dk, dv, gcol, grow, n_sub):
    b, s, _ = dq.shape
    n_chunks = s // DN_CHUNK
    n_pairs = DN_HEADS // HEADS_PER_STEP
    rows = n_sub * DN_CHUNK
    kern = functools.partial(_dn_local_kernel, n_sub=n_sub)
    spec = pl.BlockSpec((1, rows, LANES), lambda bi, p, g: (bi, g, p))
    ospec = pl.BlockSpec((1, 1, 2, rows, LANES), lambda bi, p, g: (bi, p, 0, g, 0))
    big = lambda dt: jax.ShapeDtypeStruct((b, n_pairs, 2, s, LANES), dt)
    return pl.pallas_call(
        kern,
        grid=(b, n_pairs, n_chunks // n_sub),
        in_specs=[spec, spec, spec,
                  pl.BlockSpec((1, 1, rows, 4 * HEADS_PER_STEP), lambda bi, p, g: (bi, p, g, 0)),
                  pl.BlockSpec((1, 1, n_sub, 1, PACK_LANES), lambda bi, p, g: (bi, p, g, 0, 0))],
        out_specs=[ospec, ospec, ospec, ospec, ospec,
                   pl.BlockSpec((1, 1, 2, n_sub, 1, LANES), lambda bi, p, g: (bi, p, 0, g, 0, 0))],
        out_shape=[big(f32), big(bf16), big(bf16), big(bf16), big(bf16),
                   jax.ShapeDtypeStruct((b, n_pairs, 2, n_chunks, 1, LANES), f32)],
        compiler_params=_params(("parallel", "parallel", "parallel")),
        name="dn_local",
    )(dq, dk, dv, gcol, grow)


def _dn_scan_kernel(*refs, n_sub, n_pairs):
    ins = refs[:12]
    of_ref, ob_ref, s_ref = refs[12:]
    views = (ins[0::2], ins[1::2])
    outs = (of_ref, ob_ref)
    c = DN_CHUNK
    head0 = lax.broadcasted_iota(jnp.int32, (c, LANES), 1) < DN_DK
    rs = lax.broadcasted_iota(jnp.int32, (LANES, LANES), 0) // DN_DK
    cs = lax.broadcasted_iota(jnp.int32, (LANES, LANES), 1) // DN_DV
    state_bd = rs == cs
    zero_b = jnp.zeros((c, LANES), bf16)

    @pl.when(pl.program_id(1) == 0)
    def _():
        s_ref[...] = jnp.zeros(s_ref.shape, f32)

    chains = [(dd, p) for dd in range(2) for p in range(n_pairs)]
    for t in range(n_sub):
        def rows(dd):
            tt = t if dd == 0 else n_sub - 1 - t
            return tt, slice(tt * c, (tt + 1) * c)

        def rd(dd, p, which):
            return views[dd][which][0, p, 0, rows(dd)[1], :]

        state = [s_ref[dd, p] for dd, p in chains]
        ws = [jnp.dot(jnp.concatenate([rd(dd, p, 1), rd(dd, p, 2)], axis=0), st.astype(bf16),
                      preferred_element_type=f32) for (dd, p), st in zip(chains, state)]
        v_new = [(rd(dd, p, 0) - x[:c]).astype(bf16) for (dd, p), x in zip(chains, ws)]
        upd = [_dot_tn(rd(dd, p, 3), v) for (dd, p), v in zip(chains, v_new)]
        for (dd, p), st, x in zip(chains, state, upd):
            s_ref[dd, p] = jnp.where(state_bd, st * views[dd][5][0, p, 0, rows(dd)[0]] + x, 0.0)
        for (dd, p), x, v in zip(chains, ws, v_new):
            v_bd = jnp.concatenate([jnp.where(head0, v, zero_b), jnp.where(head0, zero_b, v)], axis=0)
            outs[dd][0, rows(dd)[1], p * LANES:(p + 1) * LANES] = x[c:] + jnp.dot(
                rd(dd, p, 4), v_bd, preferred_element_type=f32)


def _dn_scan(u, w, qd, kd, qkd, eg, n_sub):
    b, n_pairs, _, s, _ = u.shape
    n_groups = s // (n_sub * DN_CHUNK)
    rows = n_sub * DN_CHUNK
    kern = functools.partial(_dn_scan_kernel, n_sub=n_sub, n_pairs=n_pairs)
    fwd = pl.BlockSpec((1, n_pairs, 1, rows, LANES), lambda bi, g: (bi, 0, 0, g, 0))
    bwd = pl.BlockSpec((1, n_pairs, 1, rows, LANES), lambda bi, g: (bi, 0, 1, n_groups - 1 - g, 0))
    egf = pl.BlockSpec((1, n_pairs, 1, n_sub, 1, LANES), lambda bi, g: (bi, 0, 0, g, 0, 0))
    egb = pl.BlockSpec((1, n_pairs, 1, n_sub, 1, LANES), lambda bi, g: (bi, 0, 1, n_groups - 1 - g, 0, 0))
    out_sd = jax.ShapeDtypeStruct((b, s, DN_V), f32)
    return pl.pallas_call(
        kern,
        grid=(b, n_groups),
        in_specs=[fwd, bwd] * 5 + [egf, egb],
        out_specs=[pl.BlockSpec((1, rows, DN_V), lambda bi, g: (bi, g, 0)),
                   pl.BlockSpec((1, rows, DN_V), lambda bi, g: (bi, n_groups - 1 - g, 0))],
        out_shape=[out_sd, out_sd],
        scratch_shapes=[pltpu.VMEM((2, n_pairs, LANES, LANES), f32)],
        compiler_params=_params(("parallel", "arbitrary")),
        name="dn_scan",
    )(u, u, w, w, qd, qd, kd, kd, qkd, qkd, eg, eg)


def _merge_kernel(oa_ref, of_ref, ob_ref, z_ref, gl_ref, x_ref, dng_ref, woa_ref, wod_ref, wout_ref, out_ref):
    ya = jnp.dot(oa_ref[...], woa_ref[...], preferred_element_type=f32)
    parts = []
    for j in range(DN_V // LANES):
        cols = slice(j * LANES, (j + 1) * LANES)
        o = of_ref[:, cols] + ob_ref[:, cols]
        ms = _group_sum(o * o, DN_DV) * (1.0 / DN_DV)
        z = z_ref[:, cols].astype(f32)
        parts.append((o * lax.rsqrt(ms + EPS) * dng_ref[...] * (z * _sigmoid(z))).astype(bf16))
    yd = jnp.dot(jnp.concatenate(parts, axis=1), wod_ref[...], preferred_element_type=f32)
    g0 = _sigmoid(gl_ref[:, :D_MODEL].astype(f32))
    g1 = _sigmoid(gl_ref[:, D_MODEL:].astype(f32))
    mixed = (g0 * ya + g1 * yd).astype(bf16)
    out_ref[...] = x_ref[...] + jnp.dot(mixed, wout_ref[...], preferred_element_type=f32)


def _merge(oa, of, ob, z, gl, x2, dng, woa, wod, wout, tm):
    t = x2.shape[0]
    row = lambda w: pl.BlockSpec((tm, w), lambda i: (i, 0))
    return pl.pallas_call(
        _merge_kernel,
        grid=(t // tm,),
        in_specs=[row(A_Q), row(DN_V), row(DN_V), row(W_Z), row(W_G), row(D_MODEL),
                  _resident((1, LANES)), _resident((A_Q, D_MODEL)), _resident((DN_V, D_MODEL)),
                  _resident((D_MODEL, D_MODEL))],
        out_specs=row(D_MODEL),
        out_shape=jax.ShapeDtypeStruct((t, D_MODEL), f32),
        compiler_params=_params(("parallel",)),
        name="merge",
    )(oa, of, ob, z, gl, x2, dng, woa, wod, wout)


def _ffn_kernel(x_ref, xprev_ref, xnext_ref, g_ref, wu_ref, cw_ref, wd_ref, out_ref,
                h_ref, ug_ref, uv_ref, *, tm, n_tiles):
    i = pl.program_id(1)
    halo = F32_SUBLANES

    def norm(x):
        ms = jnp.mean(x * x, axis=-1, keepdims=True)
        return x * lax.rsqrt(ms + EPS) * g_ref[...]

    h_ref[0:halo, :] = jnp.where(i > 0, norm(xprev_ref[0]), 0.0).astype(bf16)
    h_ref[halo:halo + tm, :] = norm(x_ref[0]).astype(bf16)
    h_ref[halo + tm:, :] = jnp.where(i < n_tiles - 1, norm(xnext_ref[0]), 0.0).astype(bf16)
    h = h_ref[...]

    def conv(u_ref, slot, col0):
        acc = None
        for kk in range(FFN_CONV_W):
            term = (u_ref[slot, pl.ds(halo - FFN_CONV_W // 2 + kk, tm), :]
                    * cw_ref[kk:kk + 1, col0:col0 + FF_CHUNK])
            acc = term if acc is None else acc + term
        return acc

    def up(j, slot):
        ug_ref[slot] = jnp.dot(h, wu_ref[:, j * FF_CHUNK:(j + 1) * FF_CHUNK], preferred_element_type=f32)
        uv_ref[slot] = jnp.dot(h, wu_ref[:, D_FF + j * FF_CHUNK:D_FF + (j + 1) * FF_CHUNK],
                               preferred_element_type=f32)

    n_ff = D_FF // FF_CHUNK
    acc = x_ref[0]
    up(0, 0)
    for j in range(n_ff):
        slot = j % 2
        if j + 1 < n_ff:
            up(j + 1, 1 - slot)
        gate = conv(ug_ref, slot, j * FF_CHUNK)
        val = conv(uv_ref, slot, D_FF + j * FF_CHUNK)
        act = (gate * _sigmoid(gate) * val).astype(bf16)
        acc = acc + jnp.dot(act, wd_ref[j], preferred_element_type=f32)
    out_ref[0] = acc


def _ffn(x3, g, wu, cw, wd, tm):
    b, s, _ = x3.shape
    n_tiles = s // tm
    halo = F32_SUBLANES
    per = tm // halo
    last = s // halo - 1
    n_ff = D_FF // FF_CHUNK
    kern = functools.partial(_ffn_kernel, tm=tm, n_tiles=n_tiles)
    return pl.pallas_call(
        kern,
        grid=(b, n_tiles),
        in_specs=[
            pl.BlockSpec((1, tm, D_MODEL), lambda bi, i: (bi, i, 0)),
            pl.BlockSpec((1, halo, D_MODEL), lambda bi, i: (bi, jnp.maximum(i * per - 1, 0), 0)),
            pl.BlockSpec((1, halo, D_MODEL), lambda bi, i: (bi, jnp.minimum((i + 1) * per, last), 0)),
            _resident((1, D_MODEL)),
            _resident((D_MODEL, 2 * D_FF)),
            _resident((F32_SUBLANES, 2 * D_FF)),
            _resident((n_ff, FF_CHUNK, D_MODEL)),
        ],
        out_specs=pl.BlockSpec((1, tm, D_MODEL), lambda bi, i: (bi, i, 0)),
        out_shape=jax.ShapeDtypeStruct((b, s, D_MODEL), f32),
        scratch_shapes=[pltpu.VMEM((tm + 2 * halo, D_MODEL), bf16),
                        pltpu.VMEM((2, tm + 2 * halo, FF_CHUNK), f32),
                        pltpu.VMEM((2, tm + 2 * halo, FF_CHUNK), f32)],
        compiler_params=_params(("parallel", "parallel")),
        name="ffn",
    )(x3, x3, x3, g, wu, cw, wd)


def _rope_tables(s):
    rows_n = s // GRID_W
    row = jnp.repeat(jnp.arange(rows_n), GRID_W).astype(f32)
    col = jnp.tile(jnp.arange(GRID_W), rows_n).astype(f32)
    inv_freq = ROPE_THETA ** (-jnp.arange(ROPE_FREQS, dtype=f32) / ROPE_FREQS)
    ang_r = row[:, None] * inv_freq
    ang_c = col[:, None] * inv_freq
    cos64 = jnp.concatenate([jnp.cos(ang_r), jnp.cos(ang_r), jnp.cos(ang_c), jnp.cos(ang_c)], axis=1)
    sin64 = jnp.concatenate([-jnp.sin(ang_r), jnp.sin(ang_r), -jnp.sin(ang_c), jnp.sin(ang_c)], axis=1)
    return jnp.tile(cos64, (1, 2)), jnp.tile(sin64, (1, 2))


def _pad_rows(w, rows):
    return jnp.pad(w, ((0, rows - w.shape[0]), (0, 0)))


def _lane_row(vals, offset):
    return jnp.pad(vals.astype(f32), (offset, LANES - offset - vals.shape[0]))[None, :]


def kernel(x, norm_mix_g, w_in, q_norm_g, k_norm_g, dn_conv_w, dn_a_log, dn_dt_bias, dn_out_norm_g,
           w_o_attn, w_o_dn, w_out, norm_ffn_g, w_up, ffn_conv_w, w_down):
    b, s, d = x.shape
    assert d == D_MODEL and s % 512 == 0 and s % GRID_W == 0
    t = b * s
    tm = 512
    n_chunks = s // DN_CHUNK
    n_pairs = DN_HEADS // HEADS_PER_STEP
    hp = HEADS_PER_STEP
    cos_t, sin_t = _rope_tables(s)
    offs = np.cumsum((A_Q, A_KV, A_KV, DN_QK, DN_QK, DN_V, 2 * DN_HEADS, 2 * DN_HEADS, DN_V))
    n_ff = D_FF // FF_CHUNK

    for l in range(DEPTH):
        wl = w_in[l]
        w_cat = jnp.concatenate(
            [wl[:, :offs[5]], wl[:, offs[7]:offs[8]], wl[:, offs[8]:], wl[:, offs[5]:offs[7]],
             jnp.zeros((D_MODEL, LANES - 4 * DN_HEADS), wl.dtype)], axis=1).astype(bf16)
        a, dd, z, gl, bg = _inproj(x.reshape(t, d), norm_mix_g[l][None, :], w_cat, tm)

        q, k, vt, dq, dk, dv, gates = _prep(
            a.reshape(b, s, W_A), dd.reshape(b, s, W_D), bg.reshape(b, s, W_BG),
            jnp.tile(q_norm_g[l], 2)[None, :], jnp.tile(k_norm_g[l], 2)[None, :], cos_t, sin_t,
            _pad_rows(dn_conv_w[l], F32_SUBLANES),
            _lane_row(dn_a_log[l].reshape(-1), 2 * DN_HEADS),
            _lane_row(dn_dt_bias[l].reshape(-1), 2 * DN_HEADS), tm)

        o_attn = _attention(q, k, vt, 256)

        g4 = gates[:, :, :4 * DN_HEADS].reshape(b, s, 4, n_pairs, hp)
        gcol = jnp.transpose(g4, (0, 3, 1, 2, 4)).reshape(b, n_pairs, s, 4 * hp)
        grow = jnp.transpose(g4[:, :, 2:].reshape(b, n_chunks, DN_CHUNK, 2, n_pairs, hp),
                             (0, 4, 1, 3, 5, 2)).reshape(b, n_pairs, n_chunks, 1, PACK_LANES)
        o_f, o_b = _dn_scan(*_dn_local(dq, dk, dv, gcol, grow, 8), 8)

        x = _merge(o_attn.reshape(t, A_Q), o_f.reshape(t, DN_V), o_b.reshape(t, DN_V), z, gl,
                   x.reshape(t, d),
                   jnp.tile(dn_out_norm_g[l], 2)[None, :], w_o_attn[l].astype(bf16),
                   w_o_dn[l].astype(bf16), w_out[l].astype(bf16), tm).reshape(b, s, d)

        x = _ffn(x, norm_ffn_g[l][None, :], w_up[l].astype(bf16),
                 _pad_rows(ffn_conv_w[l], F32_SUBLANES),
                 w_down[l].astype(bf16).reshape(n_ff, FF_CHUNK, d), tm)
    return x
```
